```python
import math
import jax
import jax.numpy as jnp
from jax import lax
import numpy as np

D_MODEL = 2048
BATCH = 1
SEQ = 8192
DEPTH = 4

D_FF = 4 * D_MODEL
CONV_A_WIDTH = D_MODEL // 2
CONV_A_K = 3
FOX_HEADS = 8
FOX_HEAD_DIM = (D_MODEL // 2) // FOX_HEADS
FOX_WIDTH = FOX_HEADS * FOX_HEAD_DIM
FOX_Q_BLOCK = 128
AB_IN = 3 * CONV_A_WIDTH + 3 * FOX_WIDTH + FOX_HEADS
AB_SPLITS = (CONV_A_WIDTH, CONV_A_WIDTH, CONV_A_WIDTH, FOX_WIDTH, FOX_WIDTH, FOX_WIDTH, FOX_HEADS)
GDN_HEAD_DIM = 128
GDN_QK_HEADS = D_MODEL // GDN_HEAD_DIM
GDN_V_HEADS = 2 * GDN_QK_HEADS
GDN_K_WIDTH = GDN_QK_HEADS * GDN_HEAD_DIM
GDN_V_WIDTH = GDN_V_HEADS * GDN_HEAD_DIM
GDN_QKV = 2 * GDN_K_WIDTH + GDN_V_WIDTH
GDN_IN = GDN_QKV + GDN_V_WIDTH + 2 * GDN_V_HEADS
GDN_CONV_K = 4
GDN_CHUNK = 64
N_EVEN = (DEPTH + 1) // 2
N_ODD = DEPTH // 2
NORM_EPS = 1e-6

kernel_name = "hybrid_conv_fox_gdn_trunk"


def _split(t, sizes):
    idx = [int(s) for s in np.cumsum(sizes)[:-1]]
    return jnp.split(t, idx, axis=-1)


def rms_norm(x, g):
    xf = x.astype(jnp.float32)
    y = xf * lax.rsqrt(jnp.mean(xf * xf, axis=-1, keepdims=True) + NORM_EPS) * g.astype(jnp.float32)
    return y.astype(x.dtype)


def l2_norm(x):
    xf = x.astype(jnp.float32)
    return xf * lax.rsqrt(jnp.sum(xf * xf, axis=-1, keepdims=True) + NORM_EPS)


def causal_depthwise_conv(u, w):
    k_width = w.shape[-1]
    rhs = jnp.transpose(w)[:, None, :].astype(u.dtype)
    return lax.conv_general_dilated(
        u, rhs, window_strides=(1,), padding=[(k_width - 1, 0)],
        dimension_numbers=("NWC", "WIO", "NWC"), feature_group_count=u.shape[-1])


def forgetting_attention(q, k, v, log_f):
    bsz, s_len, n_h, d = q.shape
    nb = s_len // FOX_Q_BLOCK
    c = jnp.cumsum(log_f.astype(jnp.float32), axis=1).transpose(0, 2, 1)
    q_blocks = q.reshape(bsz, nb, FOX_Q_BLOCK, n_h, d).transpose(1, 0, 3, 2, 4)
    c_blocks = c.reshape(bsz, n_h, nb, FOX_Q_BLOCK).transpose(2, 0, 1, 3)
    kh = k.transpose(0, 2, 1, 3)
    vh = v.transpose(0, 2, 1, 3)
    k_pos = jnp.arange(s_len)
    scale = d ** -0.5

    def block(args):
        q_blk, c_blk, blk_idx = args
        q_pos = blk_idx * FOX_Q_BLOCK + jnp.arange(FOX_Q_BLOCK)
        logits = (jnp.einsum("bhqd,bhkd->bhqk", q_blk, kh).astype(jnp.float32) * scale
                  + c_blk[..., :, None] - c[:, :, None, :])
        logits = jnp.where(k_pos[None, :] <= q_pos[:, None], logits, -jnp.inf)
        p = jax.nn.softmax(logits, axis=-1)
        return jnp.einsum("bhqk,bhkd->bhqd", p.astype(vh.dtype), vh)

    o = lax.map(block, (q_blocks, c_blocks, jnp.arange(nb)))
    return o.transpose(1, 0, 3, 2, 4).reshape(bsz, s_len, n_h * d)


def conv_fox_mixer(h, w_in, conv_w, f_bias, q_norm, k_norm, w_out):
    bsz, s_len, _ = h.shape
    proj = h @ w_in
    gate_b, gate_c, xv, q, k, v, f_logit = _split(proj, AB_SPLITS)
    y_a = gate_b * causal_depthwise_conv(gate_c * xv, conv_w)
    q = rms_norm(q.reshape(bsz, s_len, FOX_HEADS, FOX_HEAD_DIM), q_norm)
    k = rms_norm(k.reshape(bsz, s_len, FOX_HEADS, FOX_HEAD_DIM), k_norm)
    v = v.reshape(bsz, s_len, FOX_HEADS, FOX_HEAD_DIM)
    log_f = jax.nn.log_sigmoid(f_logit.astype(jnp.float32) + f_bias.astype(jnp.float32))
    y_b = forgetting_attention(q, k, v, log_f).astype(h.dtype)
    return jnp.concatenate([y_a, y_b], axis=-1) @ w_out


def gated_delta_rule_chunked(q, k, v, g, beta):
    f32 = jnp.float32
    bsz, s_len, n_h, dk = q.shape
    dv = v.shape[-1]
    csz = GDN_CHUNK
    n = s_len // csz

    def to_chunks(t):
        t = t.reshape((bsz, n, csz, n_h) + t.shape[3:])
        return jnp.moveaxis(t, 1, 0).swapaxes(2, 3)

    qc = to_chunks(q.astype(f32) * dk ** -0.5)
    kc = to_chunks(k.astype(f32))
    vc = to_chunks(v.astype(f32))
    bc = to_chunks(beta.astype(f32))
    gc = jnp.cumsum(to_chunks(g.astype(f32)), axis=-1)
    incl = jnp.tril(jnp.ones((csz, csz), dtype=bool))
    strict = jnp.tril(jnp.ones((csz, csz), dtype=bool), -1)
    diff = gc[..., :, None] - gc[..., None, :]
    decay = jnp.where(incl, jnp.exp(jnp.where(incl, diff, 0.0)), 0.0)
    kb = kc * bc[..., None]
    lower = jnp.where(strict, jnp.einsum("nbhcd,nbhsd->nbhcs", kb, kc) * decay, 0.0)
    eye = jnp.eye(csz, dtype=f32)
    t_inv = lax.linalg.triangular_solve(lower + eye, jnp.broadcast_to(eye, lower.shape),
                                        left_side=True, lower=True, unit_diagonal=True)
    u = jnp.einsum("nbhcs,nbhse->nbhce", t_inv, vc * bc[..., None])
    w = jnp.einsum("nbhcs,nbhsd->nbhcd", t_inv, kb * jnp.exp(gc)[..., None])

    def step(state, inp):
        q_i, k_i, u_i, w_i, g_i, dec_i = inp
        v_new = u_i - jnp.einsum("bhcd,bhde->bhce", w_i, state)
        scores = jnp.einsum("bhcd,bhsd->bhcs", q_i, k_i) * dec_i
        o_i = (jnp.einsum("bhcd,bhde->bhce", q_i * jnp.exp(g_i)[..., None], state)
               + jnp.einsum("bhcs,bhse->bhce", scores, v_new))
        g_last = g_i[..., -1:]
        state = (state * jnp.exp(g_last)[..., None]
                 + jnp.einsum("bhcd,bhce->bhde", k_i * jnp.exp(g_last - g_i)[..., None], v_new))
        return state, o_i

    state0 = jnp.zeros((bsz, n_h, dk, dv), f32)
    _, o = lax.scan(step, state0, (qc, kc, u, w, gc, decay))
    return jnp.moveaxis(o.swapaxes(2, 3), 0, 1).reshape(bsz, s_len, n_h, dv)


def gated_deltanet_mixer(h, w_in, conv_w, a_log, dt_bias, out_norm, w_out):
    bsz, s_len, _ = h.shape
    proj = h @ w_in
    qkv, z, b, a = _split(proj, (GDN_QKV, GDN_V_WIDTH, GDN_V_HEADS, GDN_V_HEADS))
    qkv = jax.nn.silu(causal_depthwise_conv(qkv, conv_w))
    q, k, v = _split(qkv, (GDN_K_WIDTH, GDN_K_WIDTH, GDN_V_WIDTH))
    rep = GDN_V_HEADS // GDN_QK_HEADS
    q = jnp.repeat(l2_norm(q.reshape(bsz, s_len, GDN_QK_HEADS, GDN_HEAD_DIM)), rep, axis=2)
    k = jnp.repeat(l2_norm(k.reshape(bsz, s_len, GDN_QK_HEADS, GDN_HEAD_DIM)), rep, axis=2)
    v = v.reshape(bsz, s_len, GDN_V_HEADS, GDN_HEAD_DIM)
    beta = jax.nn.sigmoid(b.astype(jnp.float32))
    g = -jnp.exp(a_log.astype(jnp.float32)) * jax.nn.softplus(a.astype(jnp.float32) + dt_bias.astype(jnp.float32))
    o = gated_delta_rule_chunked(q, k, v, g, beta)
    z = z.reshape(bsz, s_len, GDN_V_HEADS, GDN_HEAD_DIM).astype(jnp.float32)
    o = rms_norm(o, out_norm) * jax.nn.silu(z)
    return o.reshape(bsz, s_len, GDN_V_WIDTH).astype(h.dtype) @ w_out


def sqrelu_mlp(h, w1, w2):
    return jnp.square(jax.nn.relu(h @ w1)) @ w2


def setup_inputs(seed: int = 0) -> dict:
    key = jax.random.key(seed)
    ks = jax.random.split(key, 17)
    f32 = jnp.float32

    def normal(k, shape, scale):
        return jax.random.normal(k, shape, f32) * scale

    def gain(k, shape):
        return 1.0 + 0.02 * jax.random.normal(k, shape, f32)

    x = normal(ks[0], (BATCH, SEQ, D_MODEL), 1.0)
    norm_mix = gain(ks[1], (DEPTH, D_MODEL))
    norm_mlp = gain(ks[2], (DEPTH, D_MODEL))
    w_mlp_in = normal(ks[3], (DEPTH, D_MODEL, D_FF), D_MODEL ** -0.5)
    w_mlp_out = normal(ks[4], (DEPTH, D_FF, D_MODEL), D_FF ** -0.5)
    ab_w_in = normal(ks[5], (N_EVEN, D_MODEL, AB_IN), D_MODEL ** -0.5)
    ab_conv_w = normal(ks[6], (N_EVEN, CONV_A_WIDTH, CONV_A_K), CONV_A_K ** -0.5)
    fox_f_bias = jax.random.uniform(ks[7], (N_EVEN, FOX_HEADS), f32, 1.0, 5.0)
    fox_q_norm = gain(ks[8], (N_EVEN, FOX_HEAD_DIM))
    fox_k_norm = gain(ks[9], (N_EVEN, FOX_HEAD_DIM))
    ab_w_out = normal(ks[10], (N_EVEN, CONV_A_WIDTH + FOX_WIDTH, D_MODEL), (CONV_A_WIDTH + FOX_WIDTH) ** -0.5)
    gdn_w_in = normal(ks[11], (N_ODD, D_MODEL, GDN_IN), D_MODEL ** -0.5)
    gdn_conv_w = normal(ks[12], (N_ODD, GDN_QKV, GDN_CONV_K), GDN_CONV_K ** -0.5)
    gdn_a_log = jnp.log(jax.random.uniform(ks[13], (N_ODD, GDN_V_HEADS), f32, 1.0, 16.0))
    dt = jnp.exp(jax.random.uniform(ks[14], (N_ODD, GDN_V_HEADS), f32, math.log(1e-3), math.log(1e-1)))
    gdn_dt_bias = dt + jnp.log(-jnp.expm1(-dt))
    gdn_out_norm = gain(ks[15], (N_ODD, GDN_HEAD_DIM))
    gdn_w_out = normal(ks[16], (N_ODD, GDN_V_WIDTH, D_MODEL), GDN_V_WIDTH ** -0.5)
    return {
        "x": x, "norm_mix": norm_mix, "norm_mlp": norm_mlp,
        "w_mlp_in": w_mlp_in, "w_mlp_out": w_mlp_out,
        "ab_w_in": ab_w_in, "ab_conv_w": ab_conv_w, "fox_f_bias": fox_f_bias,
        "fox_q_norm": fox_q_norm, "fox_k_norm": fox_k_norm, "ab_w_out": ab_w_out,
        "gdn_w_in": gdn_w_in, "gdn_conv_w": gdn_conv_w, "gdn_a_log": gdn_a_log,
        "gdn_dt_bias": gdn_dt_bias, "gdn_out_norm": gdn_out_norm, "gdn_w_out": gdn_w_out,
    }


def reference(x, norm_mix, norm_mlp, w_mlp_in, w_mlp_out, ab_w_in, ab_conv_w, fox_f_bias,
              fox_q_norm, fox_k_norm, ab_w_out, gdn_w_in, gdn_conv_w, gdn_a_log,
              gdn_dt_bias, gdn_out_norm, gdn_w_out):
    for layer in range(DEPTH):
        i = layer // 2
        h = rms_norm(x, norm_mix[layer])
        if layer % 2 == 0:
            x = x + conv_fox_mixer(h, ab_w_in[i], ab_conv_w[i], fox_f_bias[i],
                                   fox_q_norm[i], fox_k_norm[i], ab_w_out[i])
        else:
            x = x + gated_deltanet_mixer(h, gdn_w_in[i], gdn_conv_w[i], gdn_a_log[i],
                                         gdn_dt_bias[i], gdn_out_norm[i], gdn_w_out[i])
        h = rms_norm(x, norm_mlp[layer])
        x = x + sqrelu_mlp(h, w_mlp_in[layer], w_mlp_out[layer])
    return x
```

```python
import functools

import jax
import jax.numpy as jnp
from jax import lax
from jax.experimental import pallas as pl
from jax.experimental.pallas import tpu as pltpu

F32 = jnp.float32
BF16 = jnp.bfloat16

NORM_EPS = 1e-6
FOX_HEADS = 8
CONV_A_K = 3
GDN_HEAD_DIM = 128
GDN_CONV_K = 4
GDN_CHUNK = 64
LANES = 128
HALO = 8
GROUP = 2 * GDN_CHUNK
INV_BASE = 16
VMEM_LIMIT = 56 * 1024 * 1024


def _params(semantics, vmem=VMEM_LIMIT):
    return pltpu.CompilerParams(dimension_semantics=semantics, vmem_limit_bytes=vmem)


def _dot(a, b):
    return jnp.dot(a, b, preferred_element_type=F32)


def _dot_nt(a, b):
    return lax.dot_general(a, b, (((1,), (1,)), ((), ())), preferred_element_type=F32)


def _sigmoid(x):
    return 1.0 / (1.0 + jnp.exp(-x))


def _silu(x):
    return x * _sigmoid(x)


def _softplus(x):
    return jnp.maximum(x, 0.0) + jnp.log1p(jnp.exp(-jnp.abs(x)))


def _norm_matmul_kernel(*refs, act, has_side):
    if has_side:
        x_ref, g_ref, w_ref, ws_ref, o_ref, side_ref, h_ref = refs
    else:
        x_ref, g_ref, w_ref, o_ref, h_ref = refs

    @pl.when(pl.program_id(1) == 0)
    def _():
        xf = x_ref[...]
        ms = jnp.mean(xf * xf, axis=-1, keepdims=True)
        h = (xf * lax.rsqrt(ms + NORM_EPS) * g_ref[...]).astype(BF16)
        h_ref[...] = h
        if has_side:
            side_ref[...] = _dot(h, ws_ref[...])

    acc = _dot(h_ref[...], w_ref[...])
    if act:
        acc = jnp.square(jnp.maximum(acc, 0.0))
    o_ref[...] = acc.astype(o_ref.dtype)


def _norm_matmul(x, g, w, *, act=False, side_w=None, out_dtype=BF16, tm=1024, tn=1024, name):
    m, k = x.shape
    n = w.shape[1]
    tm, tn = min(tm, m), min(tn, n)
    assert m % tm == 0 and n % tn == 0
    in_specs = [
        pl.BlockSpec((tm, k), lambda i, j: (i, 0)),
        pl.BlockSpec((1, k), lambda i, j: (0, 0)),
        pl.BlockSpec((k, tn), lambda i, j: (0, j)),
    ]
    args = [x, g.reshape(1, k), w]
    out_shape = [jax.ShapeDtypeStruct((m, n), out_dtype)]
    out_specs = [pl.BlockSpec((tm, tn), lambda i, j: (i, j))]
    if side_w is not None:
        in_specs.append(pl.BlockSpec((k, LANES), lambda i, j: (0, 0)))
        args.append(side_w)
        out_shape.append(jax.ShapeDtypeStruct((m, LANES), F32))
        out_specs.append(pl.BlockSpec((tm, LANES), lambda i, j: (i, 0)))
    res = pl.pallas_call(
        functools.partial(_norm_matmul_kernel, act=act, has_side=side_w is not None),
        out_shape=out_shape,
        grid=(m // tm, n // tn),
        in_specs=in_specs,
        out_specs=out_specs,
        scratch_shapes=[pltpu.VMEM((tm, k), BF16)],
        compiler_params=_params(("parallel", "arbitrary")),
        name=name,
    )(*args)
    return res if side_w is not None else res[0]


def _matmul_res_kernel(a_ref, w_ref, r_ref, o_ref):
    o_ref[...] = r_ref[...] + _dot(a_ref[...], w_ref[...])


def _matmul_res(a, w, res, *, tm=512, tn=512, name):
    m, k = a.shape
    n = w.shape[1]
    tm, tn = min(tm, m), min(tn, n)
    assert m % tm == 0 and n % tn == 0
    return pl.pallas_call(
        _matmul_res_kernel,
        out_shape=jax.ShapeDtypeStruct((m, n), F32),
        grid=(m // tm, n // tn),
        in_specs=[
            pl.BlockSpec((tm, k), lambda i, j: (i, 0)),
            pl.BlockSpec((k, tn), lambda i, j: (0, j)),
            pl.BlockSpec((tm, tn), lambda i, j: (i, j)),
        ],
        out_specs=pl.BlockSpec((tm, tn), lambda i, j: (i, j)),
        compiler_params=_params(("parallel", "arbitrary")),
        name=name,
    )(a, w, res)


def _fox_prep_kernel(gb_ref, gc_ref, xv_ref, q_ref, k_ref, fl_ref, cw_ref, fb_ref, qg_ref, kg_ref,
                     ya_ref, qn_ref, kn_ref, c_ref, ct_ref, ubuf, carry, *, head_dim, scale):
    tb = gb_ref.shape[0]
    first = pl.program_id(0) == 0

    @pl.when(first)
    def _():
        ubuf[0:HALO, :] = jnp.zeros((HALO, ubuf.shape[1]), F32)
        carry[...] = jnp.zeros_like(carry)

    ubuf[HALO:HALO + tb, :] = gc_ref[...].astype(F32) * xv_ref[...].astype(F32)
    y = jnp.zeros((tb, ubuf.shape[1]), F32)
    for j in range(CONV_A_K):
        y = y + cw_ref[j:j + 1, :] * ubuf[pl.ds(HALO - (CONV_A_K - 1) + j, tb), :]
    ya_ref[...] = (gb_ref[...].astype(F32) * y).astype(ya_ref.dtype)
    ubuf[0:HALO, :] = ubuf[tb:tb + HALO, :]

    for h in range(q_ref.shape[1] // head_dim):
        sl = slice(h * head_dim, (h + 1) * head_dim)
        qh = q_ref[:, sl].astype(F32)
        kh = k_ref[:, sl].astype(F32)
        qms = jnp.mean(qh * qh, axis=-1, keepdims=True)
        kms = jnp.mean(kh * kh, axis=-1, keepdims=True)
        qn_ref[:, sl] = (qh * lax.rsqrt(qms + NORM_EPS) * (qg_ref[...] * scale)).astype(qn_ref.dtype)
        kn_ref[:, sl] = (kh * lax.rsqrt(kms + NORM_EPS) * kg_ref[...]).astype(kn_ref.dtype)

    z = fl_ref[...] + fb_ref[...]
    log_f = jnp.minimum(z, 0.0) - jnp.log1p(jnp.exp(-jnp.abs(z)))
    row = lax.broadcasted_iota(jnp.int32, (tb, tb), 0)
    col = lax.broadcasted_iota(jnp.int32, (tb, tb), 1)
    tri = (col <= row).astype(F32)
    c = jnp.dot(tri, log_f, precision=lax.Precision.HIGHEST, preferred_element_type=F32) + carry[...]
    c_ref[...] = c
    ct_ref[...] = c.T[0:ct_ref.shape[0], :]
    carry[...] = c[tb - 1:tb, :]


def _fox_prep(proj, fl, conv_w_t, f_bias_p, q_norm, k_norm, *, width, tb=512):
    s = proj.shape[0]
    tb = min(tb, s)
    head_dim = width // FOX_HEADS
    col = lambda c: pl.BlockSpec((tb, width), lambda i, c=c: (i, c))
    const = lambda shape: pl.BlockSpec(shape, lambda i: (0, 0))
    return pl.pallas_call(
        functools.partial(_fox_prep_kernel, head_dim=head_dim, scale=head_dim ** -0.5),
        out_shape=[
            jax.ShapeDtypeStruct((s, width), BF16),
            jax.ShapeDtypeStruct((s, width), BF16),
            jax.ShapeDtypeStruct((s, width), BF16),
            jax.ShapeDtypeStruct((s, LANES), F32),
            jax.ShapeDtypeStruct((FOX_HEADS, s), F32),
        ],
        grid=(s // tb,),
        in_specs=[col(0), col(1), col(2), col(3), col(4),
                  pl.BlockSpec((tb, LANES), lambda i: (i, 0)),
                  const((CONV_A_K, width)), const((1, LANES)),
                  const((1, head_dim)), const((1, head_dim))],
        out_specs=[
            pl.BlockSpec((tb, width), lambda i: (i, 0)),
            pl.BlockSpec((tb, width), lambda i: (i, 0)),
            pl.BlockSpec((tb, width), lambda i: (i, 0)),
            pl.BlockSpec((tb, LANES), lambda i: (i, 0)),
            pl.BlockSpec((FOX_HEADS, tb), lambda i: (0, i)),
        ],
        scratch_shapes=[pltpu.VMEM((tb + HALO, width), F32), pltpu.VMEM((1, LANES), F32)],
        compiler_params=_params(("arbitrary",)),
        name="fox_prep",
    )(proj, proj, proj, proj, proj, fl, conv_w_t, f_bias_p, q_norm.reshape(1, -1), k_norm.reshape(1, -1))


def _fox_attn_kernel(q_ref, k_ref, v_ref, cq_ref, ck_ref, o_ref, m_ref, l_ref, acc_ref, cqc_ref, *, tq, tk):
    h = pl.program_id(0)
    qi = pl.program_id(1)
    ki = pl.program_id(2)
    last_k = (qi * tq + tq - 1) // tk

    @pl.when(ki == 0)
    def _():
        m_ref[...] = jnp.full_like(m_ref, -jnp.inf)
        l_ref[...] = jnp.zeros_like(l_ref)
        acc_ref[...] = jnp.zeros_like(acc_ref)
        lane = lax.broadcasted_iota(jnp.int32, cq_ref.shape, 1)
        cqc_ref[...] = jnp.sum(jnp.where(lane == h, cq_ref[...], 0.0), axis=-1, keepdims=True)

    def step(masked):
        s = _dot_nt(q_ref[...], k_ref[...]) - ck_ref[pl.ds(h, 1), :]
        if masked:
            q_pos = qi * tq + lax.broadcasted_iota(jnp.int32, (tq, tk), 0)
            k_pos = ki * tk + lax.broadcasted_iota(jnp.int32, (tq, tk), 1)
            s = jnp.where(k_pos <= q_pos, s, -jnp.inf)
        cq = cqc_ref[...]
        m_old = m_ref[...]
        m_new = jnp.maximum(m_old, jnp.max(s, axis=-1, keepdims=True) + cq)
        p = jnp.exp(s - (m_new - cq))
        alpha = jnp.exp(m_old - m_new)
        l_ref[...] = alpha * l_ref[...] + jnp.sum(p, axis=-1, keepdims=True)
        acc_ref[...] = alpha * acc_ref[...] + _dot(p.astype(v_ref.dtype), v_ref[...])
        m_ref[...] = m_new

    crosses_diagonal = ki * tk + tk - 1 > qi * tq

    @pl.when(jnp.logical_and(ki <= last_k, crosses_diagonal))
    def _():
        step(True)

    @pl.when(jnp.logical_not(crosses_diagonal))
    def _():
        step(False)

    @pl.when(ki == last_k)
    def _():
        o_ref[...] = (acc_ref[...] / l_ref[...]).astype(o_ref.dtype)


def _fox_attention(qn, kn, proj, v_col0, c_col, c_row, *, tq=512, tk=512):
    s, width = qn.shape
    d = width // FOX_HEADS
    tq, tk = min(tq, s), min(tk, s)
    kv_idx = lambda qi, ki: jnp.minimum(ki, (qi * tq + tq - 1) // tk)
    return pl.pallas_call(
        functools.partial(_fox_attn_kernel, tq=tq, tk=tk),
        out_shape=jax.ShapeDtypeStruct((s, width), BF16),
        grid=(FOX_HEADS, s // tq, s // tk),
        in_specs=[
            pl.BlockSpec((tq, d), lambda h, qi, ki: (qi, h)),
            pl.BlockSpec((tk, d), lambda h, qi, ki: (kv_idx(qi, ki), h)),
            pl.BlockSpec((tk, d), lambda h, qi, ki: (kv_idx(qi, ki), v_col0 + h)),
            pl.BlockSpec((tq, LANES), lambda h, qi, ki: (qi, 0)),
            pl.BlockSpec((FOX_HEADS, tk), lambda h, qi, ki: (0, kv_idx(qi, ki))),
        ],
        out_specs=pl.BlockSpec((tq, d), lambda h, qi, ki: (qi, h)),
        scratch_shapes=[pltpu.VMEM((tq, 1), F32), pltpu.VMEM((tq, 1), F32),
                        pltpu.VMEM((tq, d), F32), pltpu.VMEM((tq, 1), F32)],
        compiler_params=_params(("parallel", "parallel", "arbitrary")),
        name="fox_attention",
    )(qn, kn, proj, c_col, c_row)


def _conv_silu(buf, raw_ref, w_ref, tc):
    buf[HALO:HALO + tc, :] = raw_ref[...].astype(F32)
    y = jnp.zeros((tc, buf.shape[1]), F32)
    for j in range(GDN_CONV_K):
        y = y + w_ref[j:j + 1, :] * buf[pl.ds(HALO - (GDN_CONV_K - 1) + j, tc), :]
    buf[0:HALO, :] = buf[tc:tc + HALO, :]
    return _silu(y)


def _l2_norm(x):
    return x * lax.rsqrt(jnp.sum(x * x, axis=-1, keepdims=True) + NORM_EPS)


def _unit_lower_inverse(l_mat, eye, row, col):
    n = l_mat.shape[0]
    lb = jnp.where((row // INV_BASE) == (col // INV_BASE), l_mat, 0.0).astype(BF16)
    p = eye - lb.astype(F32)
    m = _dot(lb, lb)
    power = 2
    while power < INV_BASE:
        mb = m.astype(BF16)
        if 2 * power < INV_BASE:
            both = _dot(mb, jnp.concatenate([mb, p.astype(BF16)], axis=1))
            m, x = both[:, :n], both[:, n:]
        else:
            x = _dot(mb, p.astype(BF16))
        p = p + x
        power *= 2
    size = INV_BASE
    while size < GDN_CHUNK:
        below = jnp.logical_and(row // (2 * size) == col // (2 * size), row // size == col // size + 1)
        c_off = jnp.where(below, l_mat, 0.0).astype(BF16)
        pb = p.astype(BF16)
        p = p - _dot(pb, _dot(c_off, pb).astype(BF16))
        size *= 2
    return p


def _gdn_kernel(q_ref, k_ref, v_ref, z_ref, side_ref, wq_ref, wk_ref, wv_ref, alog_ref, dtb_ref, on_ref,
                o_ref, qbuf, kbuf, vbuf, state, *, n_vheads):
    tc = q_ref.shape[0]
    d = GDN_HEAD_DIM
    hq = pl.program_id(0)
    rep = v_ref.shape[1] // d

    @pl.when(pl.program_id(1) == 0)
    def _():
        qbuf[0:HALO, :] = jnp.zeros((HALO, qbuf.shape[1]), F32)
        kbuf[0:HALO, :] = jnp.zeros((HALO, kbuf.shape[1]), F32)
        vbuf[0:HALO, :] = jnp.zeros((HALO, vbuf.shape[1]), F32)
        state[...] = jnp.zeros_like(state)

    q = _l2_norm(_conv_silu(qbuf, q_ref, wq_ref, tc)) * (d ** -0.5)
    k = _l2_norm(_conv_silu(kbuf, k_ref, wk_ref, tc))
    v = _conv_silu(vbuf, v_ref, wv_ref, tc)

    side = side_ref[...]
    beta_all = _sigmoid(side)
    g_all = -jnp.exp(alog_ref[...]) * _softplus(side + dtb_ref[...])
    lane = lax.broadcasted_iota(jnp.int32, side.shape, 1)

    row = lax.broadcasted_iota(jnp.int32, (GROUP, GROUP), 0)
    col = lax.broadcasted_iota(jnp.int32, (GROUP, GROUP), 1)
    same_chunk = (row // GDN_CHUNK) == (col // GDN_CHUNK)
    incl = jnp.logical_and(same_chunk, col <= row)
    strict = jnp.logical_and(same_chunk, col < row)
    tri = incl.astype(F32)
    eye = (row == col).astype(F32)

    for j in range(rep):
        vh = hq * rep + j
        beta = jnp.sum(jnp.where(lane == vh, beta_all, 0.0), axis=-1, keepdims=True)
        g = jnp.sum(jnp.where(lane == n_vheads + vh, g_all, 0.0), axis=-1, keepdims=True)
        s_mat = state[j]
        outs = []
        for r in range(tc // GROUP):
            rows = slice(r * GROUP, (r + 1) * GROUP)
            kg, qg, vg = k[rows], q[rows], v[rows, j * d:(j + 1) * d]
            bg = beta[rows]
            gcb = jnp.dot(tri, jnp.broadcast_to(g[rows], (GROUP, GROUP)),
                          precision=lax.Precision.HIGHEST, preferred_element_type=F32)
            gcr = gcb.T
            decay = jnp.where(incl, jnp.exp(jnp.where(incl, gcb - gcr, 0.0)), 0.0)
            kb = kg * bg
            kgb = kg.astype(BF16)
            l_mat = jnp.where(strict, _dot_nt(kb.astype(BF16), kgb) * decay, 0.0)
            t_inv = _unit_lower_inverse(l_mat, eye, row, col)
            egc = jnp.exp(gcb)
            uw = _dot(t_inv.astype(BF16),
                      jnp.concatenate([(vg * bg).astype(BF16), (kb * egc).astype(BF16)], axis=1))
            u, w = uw[:, :d], uw[:, d:]
            scores = (_dot_nt(qg.astype(BF16), kgb) * decay).astype(BF16)
            qe = qg * egc
            v_new = jnp.zeros((GROUP, d), F32)
            for c in range(GROUP // GDN_CHUNK):
                crow = slice(c * GDN_CHUNK, (c + 1) * GDN_CHUNK)
                g_last = gcb[(c + 1) * GDN_CHUNK - 1:(c + 1) * GDN_CHUNK, :]
                sb = s_mat.astype(BF16)
                ws_qs = _dot(jnp.concatenate([w[crow], qe[crow]], axis=0).astype(BF16), sb)
                in_chunk = (row // GDN_CHUNK) == c
                v_new_c = u[crow] - ws_qs[:GDN_CHUNK]
                v_new = (jnp.concatenate([v_new_c, v_new[GDN_CHUNK:]], axis=0) if c == 0
                         else jnp.concatenate([v_new[:GDN_CHUNK], v_new_c], axis=0))
                o_c = ws_qs[GDN_CHUNK:] + _dot(scores[crow], v_new.astype(BF16))
                outs.append(o_c)
                kd = jnp.where(in_chunk, kg * jnp.exp(jnp.where(in_chunk, g_last - gcb, 0.0)), 0.0)
                s_mat = s_mat * jnp.exp(g_last) + _dot(kd.T.astype(BF16), v_new.astype(BF16))
        state[j] = s_mat
        o = jnp.concatenate(outs, axis=0)
        ms = jnp.mean(o * o, axis=-1, keepdims=True)
        zj = z_ref[:, j * d:(j + 1) * d].astype(F32)
        o_ref[:, j * d:(j + 1) * d] = (o * lax.rsqrt(ms + NORM_EPS) * on_ref[...] * _silu(zj)).astype(o_ref.dtype)


def _gdn_core(proj, side, conv_w_t, alog_p, dtb_p, out_norm, *, n_qk, n_v, tc=256):
    s = proj.shape[0]
    d = GDN_HEAD_DIM
    rep = n_v // n_qk
    tc = min(tc, s)
    assert s % tc == 0 and tc % GROUP == 0
    kw = n_qk * d
    const = lambda shape: pl.BlockSpec(shape, lambda h, i: (0, 0))
    return pl.pallas_call(
        functools.partial(_gdn_kernel, n_vheads=n_v),
        out_shape=jax.ShapeDtypeStruct((s, n_v * d), BF16),
        grid=(n_qk, s // tc),
        in_specs=[
            pl.BlockSpec((tc, d), lambda h, i: (i, h)),
            pl.BlockSpec((tc, d), lambda h, i: (i, n_qk + h)),
            pl.BlockSpec((tc, rep * d), lambda h, i: (i, 2 * kw // (rep * d) + h)),
            pl.BlockSpec((tc, rep * d), lambda h, i: (i, (2 * kw + n_v * d) // (rep * d) + h)),
            pl.BlockSpec((tc, LANES), lambda h, i: (i, 0)),
            pl.BlockSpec((GDN_CONV_K, d), lambda h, i: (0, h)),
            pl.BlockSpec((GDN_CONV_K, d), lambda h, i: (0, n_qk + h)),
            pl.BlockSpec((GDN_CONV_K, rep * d), lambda h, i: (0, 2 * kw // (rep * d) + h)),
            const((1, LANES)), const((1, LANES)), const((1, d)),
        ],
        out_specs=pl.BlockSpec((tc, rep * d), lambda h, i: (i, h)),
        scratch_shapes=[pltpu.VMEM((tc + HALO, d), F32), pltpu.VMEM((tc + HALO, d), F32),
                        pltpu.VMEM((tc + HALO, rep * d), F32), pltpu.VMEM((rep, d, d), F32)],
        compiler_params=_params(("parallel", "arbitrary")),
        name="gdn_core",
    )(proj, proj, proj, proj, side, conv_w_t, conv_w_t, conv_w_t, alog_p, dtb_p, out_norm.reshape(1, d))


def _pad_lanes(vec, offset=0):
    return jnp.zeros((1, LANES), F32).at[0, offset:offset + vec.shape[0]].set(vec.astype(F32))


def kernel(x, norm_mix, norm_mlp, w_mlp_in, w_mlp_out, ab_w_in, ab_conv_w, fox_f_bias, fox_q_norm,
           fox_k_norm, ab_w_out, gdn_w_in, gdn_conv_w, gdn_a_log, gdn_dt_bias, gdn_out_norm, gdn_w_out):
    bsz, s_len, d_model = x.shape
    depth = norm_mix.shape[0]
    width = ab_conv_w.shape[1]
    n_v = gdn_a_log.shape[1]
    qkv = gdn_conv_w.shape[1]
    n_qk = (qkv - n_v * GDN_HEAD_DIM) // (2 * GDN_HEAD_DIM)
    ab_main = 6 * width
    gdn_main = qkv + n_v * GDN_HEAD_DIM
    assert ab_w_in.shape[2] == ab_main + FOX_HEADS and gdn_w_in.shape[2] == gdn_main + 2 * n_v
    assert 2 * n_v <= LANES and FOX_HEADS <= LANES

    def side_weight(w, start):
        cols = w[:, start:]
        return jnp.pad(cols, ((0, 0), (0, LANES - cols.shape[1]))).astype(BF16)

    outs = []
    for b in range(bsz):
        xb = x[b]
        for layer in range(depth):
            i = layer // 2
            if layer % 2 == 0:
                proj, fl = _norm_matmul(xb, norm_mix[layer], ab_w_in[i][:, :ab_main].astype(BF16),
                                        side_w=side_weight(ab_w_in[i], ab_main), name="ab_in_proj")
                y_a, qn, kn, c_col, c_row = _fox_prep(
                    proj, fl, ab_conv_w[i].T, _pad_lanes(fox_f_bias[i]), fox_q_norm[i], fox_k_norm[i], width=width)
                y_b = _fox_attention(qn, kn, proj, 5 * FOX_HEADS, c_col, c_row)
                mixed = jnp.concatenate([y_a, y_b], axis=-1)
                xb = _matmul_res(mixed, ab_w_out[i].astype(BF16), xb, name="ab_out_proj")
            else:
                proj, side = _norm_matmul(xb, norm_mix[layer], gdn_w_in[i][:, :gdn_main].astype(BF16),
                                          side_w=side_weight(gdn_w_in[i], gdn_main), name="gdn_in_proj")
                o = _gdn_core(proj, side, gdn_conv_w[i].T, _pad_lanes(gdn_a_log[i], n_v),
                              _pad_lanes(gdn_dt_bias[i], n_v), gdn_out_norm[i], n_qk=n_qk, n_v=n_v)
                xb = _matmul_res(o, gdn_w_out[i].astype(BF16), xb, name="gdn_out_proj")
            hidden = _norm_matmul(xb, norm_mlp[layer], w_mlp_in[layer].astype(BF16), act=True, name="mlp_in")
            xb = _matmul_res(hidden, w_mlp_out[layer].astype(BF16), xb, name="mlp_out")
        outs.append(xb)
    return jnp.stack(outs, axis=0)
```

```python
import functools

import jax
import jax.numpy as jnp
from jax import lax
from jax.experimental import pallas as pl
from jax.experimental.pallas import tpu as pltpu

F32 = jnp.float32
BF16 = jnp.bfloat16

NORM_EPS = 1e-6
FOX_HEADS = 8
CONV_A_K = 3
GDN_HEAD_DIM = 128
GDN_CONV_K = 4
GDN_CHUNK = 64
LANES = 128
HALO = 8
GROUP = 2 * GDN_CHUNK
INV_BASE = 16
VMEM_LIMIT = 56 * 1024 * 1024


def _params(semantics, vmem=VMEM_LIMIT):
    return pltpu.CompilerParams(dimension_semantics=semantics, vmem_limit_bytes=vmem)


def _dot(a, b):
    return jnp.dot(a, b, preferred_element_type=F32)


def _dot_nt(a, b):
    return lax.dot_general(a, b, (((1,), (1,)), ((), ())), preferred_element_type=F32)


def _sigmoid(x):
    return 1.0 / (1.0 + jnp.exp(-x))


def _silu(x):
    return x * _sigmoid(x)


def _softplus(x):
    return jnp.maximum(x, 0.0) + jnp.log1p(jnp.exp(-jnp.abs(x)))


def _norm_matmul_kernel(*refs, act, has_side):
    if has_side:
        x_ref, g_ref, w_ref, ws_ref, o_ref, side_ref, h_ref = refs
    else:
        x_ref, g_ref, w_ref, o_ref, h_ref = refs

    @pl.when(pl.program_id(1) == 0)
    def _():
        xf = x_ref[...]
        ms = jnp.mean(xf * xf, axis=-1, keepdims=True)
        h = (xf * lax.rsqrt(ms + NORM_EPS) * g_ref[...]).astype(BF16)
        h_ref[...] = h
        if has_side:
            side_ref[...] = _dot(h, ws_ref[...])

    acc = _dot(h_ref[...], w_ref[...])
    if act:
        acc = jnp.square(jnp.maximum(acc, 0.0))
    o_ref[...] = acc.astype(o_ref.dtype)


def _norm_matmul(x, g, w, *, act=False, side_w=None, out_dtype=BF16, tm=1024, tn=1024, name):
    m, k = x.shape
    n = w.shape[1]
    tm, tn = min(tm, m), min(tn, n)
    assert m % tm == 0 and n % tn == 0
    in_specs = [
        pl.BlockSpec((tm, k), lambda i, j: (i, 0)),
        pl.BlockSpec((1, k), lambda i, j: (0, 0)),
        pl.BlockSpec((k, tn), lambda i, j: (0, j)),
    ]
    args = [x, g.reshape(1, k), w]
    out_shape = [jax.ShapeDtypeStruct((m, n), out_dtype)]
    out_specs = [pl.BlockSpec((tm, tn), lambda i, j: (i, j))]
    if side_w is not None:
        in_specs.append(pl.BlockSpec((k, LANES), lambda i, j: (0, 0)))
        args.append(side_w)
        out_shape.append(jax.ShapeDtypeStruct((m, LANES), F32))
        out_specs.append(pl.BlockSpec((tm, LANES), lambda i, j: (i, 0)))
    res = pl.pallas_call(
        functools.partial(_norm_matmul_kernel, act=act, has_side=side_w is not None),
        out_shape=out_shape,
        grid=(m // tm, n // tn),
        in_specs=in_specs,
        out_specs=out_specs,
        scratch_shapes=[pltpu.VMEM((tm, k), BF16)],
        compiler_params=_params(("parallel", "arbitrary")),
        name=name,
    )(*args)
    return res if side_w is not None else res[0]


def _matmul_res_kernel(a_ref, w_ref, r_ref, o_ref):
    o_ref[...] = r_ref[...] + _dot(a_ref[...], w_ref[...])


def _matmul_res(a, w, res, *, tm=512, tn=512, name):
    m, k = a.shape
    n = w.shape[1]
    tm, tn = min(tm, m), min(tn, n)
    assert m % tm == 0 and n % tn == 0
    return pl.pallas_call(
        _matmul_res_kernel,
        out_shape=jax.ShapeDtypeStruct((m, n), F32),
        grid=(m // tm, n // tn),
        in_specs=[
            pl.BlockSpec((tm, k), lambda i, j: (i, 0)),
            pl.BlockSpec((k, tn), lambda i, j: (0, j)),
            pl.BlockSpec((tm, tn), lambda i, j: (i, j)),
        ],
        out_specs=pl.BlockSpec((tm, tn), lambda i, j: (i, j)),
        compiler_params=_params(("parallel", "arbitrary")),
        name=name,
    )(a, w, res)


def _fox_prep_kernel(gb_ref, gc_ref, xv_ref, q_ref, k_ref, fl_ref, cw_ref, fb_ref, qg_ref, kg_ref,
                     ya_ref, qn_ref, kn_ref, c_ref, ct_ref, ubuf, carry, *, head_dim, scale):
    tb = gb_ref.shape[0]
    first = pl.program_id(0) == 0

    @pl.when(first)
    def _():
        ubuf[0:HALO, :] = jnp.zeros((HALO, ubuf.shape[1]), F32)
        carry[...] = jnp.zeros_like(carry)

    ubuf[HALO:HALO + tb, :] = gc_ref[...].astype(F32) * xv_ref[...].astype(F32)
    y = jnp.zeros((tb, ubuf.shape[1]), F32)
    for j in range(CONV_A_K):
        y = y + cw_ref[j:j + 1, :] * ubuf[pl.ds(HALO - (CONV_A_K - 1) + j, tb), :]
    ya_ref[...] = (gb_ref[...].astype(F32) * y).astype(ya_ref.dtype)
    ubuf[0:HALO, :] = ubuf[tb:tb + HALO, :]

    for h in range(q_ref.shape[1] // head_dim):
        sl = slice(h * head_dim, (h + 1) * head_dim)
        qh = q_ref[:, sl].astype(F32)
        kh = k_ref[:, sl].astype(F32)
        qms = jnp.mean(qh * qh, axis=-1, keepdims=True)
        kms = jnp.mean(kh * kh, axis=-1, keepdims=True)
        qn_ref[:, sl] = (qh * lax.rsqrt(qms + NORM_EPS) * (qg_ref[...] * scale)).astype(qn_ref.dtype)
        kn_ref[:, sl] = (kh * lax.rsqrt(kms + NORM_EPS) * kg_ref[...]).astype(kn_ref.dtype)

    z = fl_ref[...] + fb_ref[...]
    log_f = jnp.minimum(z, 0.0) - jnp.log1p(jnp.exp(-jnp.abs(z)))
    row = lax.broadcasted_iota(jnp.int32, (tb, tb), 0)
    col = lax.broadcasted_iota(jnp.int32, (tb, tb), 1)
    tri = (col <= row).astype(F32)
    c = jnp.dot(tri, log_f, precision=lax.Precision.HIGHEST, preferred_element_type=F32) + carry[...]
    c_ref[...] = c
    ct_ref[...] = c.T[0:ct_ref.shape[0], :]
    carry[...] = c[tb - 1:tb, :]


def _fox_prep(proj, fl, conv_w_t, f_bias_p, q_norm, k_norm, *, width, tb=512):
    s = proj.shape[0]
    tb = min(tb, s)
    head_dim = width // FOX_HEADS
    col = lambda c: pl.BlockSpec((tb, width), lambda i, c=c: (i, c))
    const = lambda shape: pl.BlockSpec(shape, lambda i: (0, 0))
    return pl.pallas_call(
        functools.partial(_fox_prep_kernel, head_dim=head_dim, scale=head_dim ** -0.5),
        out_shape=[
            jax.ShapeDtypeStruct((s, width), BF16),
            jax.ShapeDtypeStruct((s, width), BF16),
            jax.ShapeDtypeStruct((s, width), BF16),
            jax.ShapeDtypeStruct((s, LANES), F32),
            jax.ShapeDtypeStruct((FOX_HEADS, s), F32),
        ],
        grid=(s // tb,),
        in_specs=[col(0), col(1), col(2), col(3), col(4),
                  pl.BlockSpec((tb, LANES), lambda i: (i, 0)),
                  const((CONV_A_K, width)), const((1, LANES)),
                  const((1, head_dim)), const((1, head_dim))],
        out_specs=[
            pl.BlockSpec((tb, width), lambda i: (i, 0)),
            pl.BlockSpec((tb, width), lambda i: (i, 0)),
            pl.BlockSpec((tb, width), lambda i: (i, 0)),
            pl.BlockSpec((tb, LANES), lambda i: (i, 0)),
            pl.BlockSpec((FOX_HEADS, tb), lambda i: (0, i)),
        ],
        scratch_shapes=[pltpu.VMEM((tb + HALO, width), F32), pltpu.VMEM((1, LANES), F32)],
        compiler_params=_params(("arbitrary",)),
        name="fox_prep",
    )(proj, proj, proj, proj, proj, fl, conv_w_t, f_bias_p, q_norm.reshape(1, -1), k_norm.reshape(1, -1))


def _fox_attn_kernel(q_ref, k_ref, v_ref, cq_ref, ck_ref, o_ref, m_ref, l_ref, acc_ref, cqc_ref, *, tq, tk):
    h = pl.program_id(0)
    qi = pl.program_id(1)
    ki = pl.program_id(2)
    last_k = (qi * tq + tq - 1) // tk

    @pl.when(ki == 0)
    def _():
        m_ref[...] = jnp.full_like(m_ref, -jnp.inf)
        l_ref[...] = jnp.zeros_like(l_ref)
        acc_ref[...] = jnp.zeros_like(acc_ref)
        lane = lax.broadcasted_iota(jnp.int32, cq_ref.shape, 1)
        cqc_ref[...] = jnp.sum(jnp.where(lane == h, cq_ref[...], 0.0), axis=-1, keepdims=True)

    def step(masked):
        s = _dot_nt(q_ref[...], k_ref[...]) - ck_ref[pl.ds(h, 1), :]
        if masked:
            q_pos = qi * tq + lax.broadcasted_iota(jnp.int32, (tq, tk), 0)
            k_pos = ki * tk + lax.broadcasted_iota(jnp.int32, (tq, tk), 1)
            s = jnp.where(k_pos <= q_pos, s, -jnp.inf)
        cq = cqc_ref[...]
        m_old = m_ref[...]
        m_new = jnp.maximum(m_old, jnp.max(s, axis=-1, keepdims=True) + cq)
        p = jnp.exp(s - (m_new - cq))
        alpha = jnp.exp(m_old - m_new)
        l_ref[...] = alpha * l_ref[...] + jnp.sum(p, axis=-1, keepdims=True)
        acc_ref[...] = alpha * acc_ref[...] + _dot(p.astype(v_ref.dtype), v_ref[...])
        m_ref[...] = m_new

    crosses_diagonal = ki * tk + tk - 1 > qi * tq

    @pl.when(jnp.logical_and(ki <= last_k, crosses_diagonal))
    def _():
        step(True)

    @pl.when(jnp.logical_not(crosses_diagonal))
    def _():
        step(False)

    @pl.when(ki == last_k)
    def _():
        o_ref[...] = (acc_ref[...] / l_ref[...]).astype(o_ref.dtype)


def _fox_attention(qn, kn, proj, v_col0, c_col, c_row, *, tq=512, tk=512):
    s, width = qn.shape
    d = width // FOX_HEADS
    tq, tk = min(tq, s), min(tk, s)
    kv_idx = lambda qi, ki: jnp.minimum(ki, (qi * tq + tq - 1) // tk)
    return pl.pallas_call(
        functools.partial(_fox_attn_kernel, tq=tq, tk=tk),
        out_shape=jax.ShapeDtypeStruct((s, width), BF16),
        grid=(FOX_HEADS, s // tq, s // tk),
        in_specs=[
            pl.BlockSpec((tq, d), lambda h, qi, ki: (qi, h)),
            pl.BlockSpec((tk, d), lambda h, qi, ki: (kv_idx(qi, ki), h)),
            pl.BlockSpec((tk, d), lambda h, qi, ki: (kv_idx(qi, ki), v_col0 + h)),
            pl.BlockSpec((tq, LANES), lambda h, qi, ki: (qi, 0)),
            pl.BlockSpec((FOX_HEADS, tk), lambda h, qi, ki: (0, kv_idx(qi, ki))),
        ],
        out_specs=pl.BlockSpec((tq, d), lambda h, qi, ki: (qi, h)),
        scratch_shapes=[pltpu.VMEM((tq, 1), F32), pltpu.VMEM((tq, 1), F32),
                        pltpu.VMEM((tq, d), F32), pltpu.VMEM((tq, 1), F32)],
        compiler_params=_params(("parallel", "parallel", "arbitrary")),
        name="fox_attention",
    )(qn, kn, proj, c_col, c_row)


def _gdn_gates_kernel(side_ref, alog_ref, dtb_ref, beta_ref, gc_ref):
    side = side_ref[...]
    tb = side.shape[0]
    beta_ref[...] = _sigmoid(side)
    g = -jnp.exp(alog_ref[...]) * _softplus(side + dtb_ref[...])
    row = lax.broadcasted_iota(jnp.int32, (tb, tb), 0)
    col = lax.broadcasted_iota(jnp.int32, (tb, tb), 1)
    tri = jnp.logical_and(row // GDN_CHUNK == col // GDN_CHUNK, col <= row).astype(F32)
    gc_ref[...] = jnp.dot(tri, g, precision=lax.Precision.HIGHEST, preferred_element_type=F32)


def _gdn_gates(side, alog_p, dtb_p, *, tb=512):
    s = side.shape[0]
    tb = min(tb, s)
    assert s % tb == 0 and tb % GDN_CHUNK == 0
    blk = pl.BlockSpec((tb, LANES), lambda i: (i, 0))
    const = pl.BlockSpec((1, LANES), lambda i: (0, 0))
    return pl.pallas_call(
        _gdn_gates_kernel,
        out_shape=[jax.ShapeDtypeStruct((s, LANES), F32), jax.ShapeDtypeStruct((s, LANES), F32)],
        grid=(s // tb,),
        in_specs=[blk, const, const],
        out_specs=[blk, blk],
        compiler_params=_params(("parallel",)),
        name="gdn_gates",
    )(side, alog_p, dtb_p)


def _conv_silu(buf, raw_ref, w_ref, tc):
    buf[HALO:HALO + tc, :] = raw_ref[...].astype(F32)
    y = jnp.zeros((tc, buf.shape[1]), F32)
    for j in range(GDN_CONV_K):
        y = y + w_ref[j:j + 1, :] * buf[pl.ds(HALO - (GDN_CONV_K - 1) + j, tc), :]
    buf[0:HALO, :] = buf[tc:tc + HALO, :]
    return _silu(y)


def _l2_norm(x):
    return x * lax.rsqrt(jnp.sum(x * x, axis=-1, keepdims=True) + NORM_EPS)


def _unit_lower_inverses(l_mats, eye, row, col):
    n = eye.shape[0]
    same_base = (row // INV_BASE) == (col // INV_BASE)
    lbs = [jnp.where(same_base, l, 0.0).astype(BF16) for l in l_mats]
    ps = [eye - lb.astype(F32) for lb in lbs]
    ms = [_dot(lb, lb) for lb in lbs]
    power = 2
    while power < INV_BASE:
        mbs = [m.astype(BF16) for m in ms]
        if 2 * power < INV_BASE:
            boths = [_dot(mb, jnp.concatenate([mb, p.astype(BF16)], axis=1)) for mb, p in zip(mbs, ps)]
            ms = [b[:, :n] for b in boths]
            xs = [b[:, n:] for b in boths]
        else:
            xs = [_dot(mb, p.astype(BF16)) for mb, p in zip(mbs, ps)]
        ps = [p + x for p, x in zip(ps, xs)]
        power *= 2
    size = INV_BASE
    while size < GDN_CHUNK:
        below = jnp.logical_and(row // (2 * size) == col // (2 * size), row // size == col // size + 1)
        c_offs = [jnp.where(below, l, 0.0).astype(BF16) for l in l_mats]
        pbs = [p.astype(BF16) for p in ps]
        ts = [_dot(c, pb).astype(BF16) for c, pb in zip(c_offs, pbs)]
        ps = [p - _dot(pb, t) for p, pb, t in zip(ps, pbs, ts)]
        size *= 2
    return ps


def _gdn_prep_kernel(q_ref, k_ref, v_ref, beta_ref, gc_ref, wq_ref, wk_ref, wv_ref,
                     g_out, n_out, qp_out, op_out, eg_out, qbuf, kbuf, vbuf, *, n_vheads):
    tc = q_ref.shape[0]
    d = GDN_HEAD_DIM
    hq = pl.program_id(0)
    rep = v_ref.shape[1] // d
    chunks_per_group = GROUP // GDN_CHUNK

    @pl.when(pl.program_id(1) == 0)
    def _():
        qbuf[0:HALO, :] = jnp.zeros((HALO, qbuf.shape[1]), F32)
        kbuf[0:HALO, :] = jnp.zeros((HALO, kbuf.shape[1]), F32)
        vbuf[0:HALO, :] = jnp.zeros((HALO, vbuf.shape[1]), F32)

    q = _l2_norm(_conv_silu(qbuf, q_ref, wq_ref, tc)) * (d ** -0.5)
    k = _l2_norm(_conv_silu(kbuf, k_ref, wk_ref, tc))
    v = _conv_silu(vbuf, v_ref, wv_ref, tc)

    beta_all = beta_ref[...]
    gc_all = gc_ref[...]
    lane = lax.broadcasted_iota(jnp.int32, beta_all.shape, 1)
    row = lax.broadcasted_iota(jnp.int32, (GROUP, GROUP), 0)
    col = lax.broadcasted_iota(jnp.int32, (GROUP, GROUP), 1)
    same_chunk = (row // GDN_CHUNK) == (col // GDN_CHUNK)
    incl = jnp.logical_and(same_chunk, col <= row)
    strict = jnp.logical_and(same_chunk, col < row)
    eye = (row == col).astype(F32)

    heads = []
    for j in range(rep):
        vh = hq * rep + j
        beta = jnp.sum(jnp.where(lane == vh, beta_all, 0.0), axis=-1, keepdims=True)
        gc = jnp.sum(jnp.where(lane == n_vheads + vh, gc_all, 0.0), axis=-1, keepdims=True)
        heads.append((beta, gc))

    probs = [(j, r) for j in range(rep) for r in range(tc // GROUP)]
    rows_of = lambda r: slice(r * GROUP, (r + 1) * GROUP)
    kgs = [k[rows_of(r)] for _, r in probs]
    qgs = [q[rows_of(r)] for _, r in probs]
    kgbs = [kg.astype(BF16) for kg in kgs]
    bgs = [heads[j][0][rows_of(r)] for j, r in probs]
    gcbs = [jnp.broadcast_to(heads[j][1][rows_of(r)], (GROUP, GROUP)) for j, r in probs]
    gcrs = [gcb.T for gcb in gcbs]
    decays = [jnp.where(incl, jnp.exp(jnp.where(incl, gcb - gcr, 0.0)), 0.0) for gcb, gcr in zip(gcbs, gcrs)]
    kbs = [kg * bg for kg, bg in zip(kgs, bgs)]
    kqs = [_dot_nt(jnp.concatenate([kb, qg], axis=0).astype(BF16), kgb) for kb, qg, kgb in zip(kbs, qgs, kgbs)]
    l_mats = [jnp.where(strict, kq[:GROUP] * dec, 0.0) for kq, dec in zip(kqs, decays)]
    scores = [(kq[GROUP:] * dec).astype(BF16) for kq, dec in zip(kqs, decays)]
    t_invs = _unit_lower_inverses(l_mats, eye, row, col)
    egcs = [jnp.exp(gcb) for gcb in gcbs]
    rhs = [jnp.concatenate([(v[rows_of(r), j * d:(j + 1) * d] * bg).astype(BF16), (kb * egc).astype(BF16)], axis=1)
           for (j, r), bg, kb, egc in zip(probs, bgs, kbs, egcs)]
    uws = [_dot(t.astype(BF16), x) for t, x in zip(t_invs, rhs)]
    wus = [jnp.concatenate([uw[:, d:], uw[:, :d]], axis=1).astype(BF16) for uw in uws]
    g_lasts = [jnp.where(row < GDN_CHUNK, gcb[GDN_CHUNK - 1:GDN_CHUNK, :], gcb[GROUP - 1:GROUP, :]) for gcb in gcbs]
    kdts = [(kg * jnp.exp(gl - gcb)).T for kg, gl, gcb in zip(kgs, g_lasts, gcbs)]
    qos = [_dot(sc, wu) for sc, wu in zip(scores, wus)]
    gns = [[_dot(jnp.where(col // GDN_CHUNK == c, kdt, 0.0).astype(BF16), wu) for c in range(chunks_per_group)]
           for kdt, wu in zip(kdts, wus)]

    for p, (j, r) in enumerate(probs):
        cols = slice(j * d, (j + 1) * d)
        qp_out[rows_of(r), cols] = (qgs[p] * egcs[p] - qos[p][:, :d]).astype(qp_out.dtype)
        op_out[rows_of(r), cols] = qos[p][:, d:].astype(op_out.dtype)
        for c in range(chunks_per_group):
            chunk = r * chunks_per_group + c
            g_out[j, chunk * d:(chunk + 1) * d, :] = gns[p][c][:, :d].astype(g_out.dtype)
            n_out[j, chunk * d:(chunk + 1) * d, :] = gns[p][c][:, d:].astype(n_out.dtype)
            last = (c + 1) * GDN_CHUNK - 1
            eg_out[j, chunk:chunk + 1, :] = jnp.exp(gcbs[p][last:last + 1, :])


def _gdn_prep(proj, beta, gc, conv_w_t, *, n_qk, n_v, tc=512):
    s = proj.shape[0]
    d = GDN_HEAD_DIM
    rep = n_v // n_qk
    tc = min(tc, s)
    assert s % tc == 0 and tc % GROUP == 0
    kw = n_qk * d
    v_blk0 = 2 * kw // (rep * d)
    n_chunks = s // GDN_CHUNK
    cpb = tc // GDN_CHUNK
    return pl.pallas_call(
        functools.partial(_gdn_prep_kernel, n_vheads=n_v),
        out_shape=[
            jax.ShapeDtypeStruct((n_v, n_chunks * d, d), BF16),
            jax.ShapeDtypeStruct((n_v, n_chunks * d, d), BF16),
            jax.ShapeDtypeStruct((s, n_v * d), BF16),
            jax.ShapeDtypeStruct((s, n_v * d), BF16),
            jax.ShapeDtypeStruct((n_v, n_chunks, LANES), F32),
        ],
        grid=(n_qk, s // tc),
        in_specs=[
            pl.BlockSpec((tc, d), lambda h, i: (i, h)),
            pl.BlockSpec((tc, d), lambda h, i: (i, n_qk + h)),
            pl.BlockSpec((tc, rep * d), lambda h, i: (i, v_blk0 + h)),
            pl.BlockSpec((tc, LANES), lambda h, i: (i, 0)),
            pl.BlockSpec((tc, LANES), lambda h, i: (i, 0)),
            pl.BlockSpec((GDN_CONV_K, d), lambda h, i: (0, h)),
            pl.BlockSpec((GDN_CONV_K, d), lambda h, i: (0, n_qk + h)),
            pl.BlockSpec((GDN_CONV_K, rep * d), lambda h, i: (0, v_blk0 + h)),
        ],
        out_specs=[
            pl.BlockSpec((rep, cpb * d, d), lambda h, i: (h, i, 0)),
            pl.BlockSpec((rep, cpb * d, d), lambda h, i: (h, i, 0)),
            pl.BlockSpec((tc, rep * d), lambda h, i: (i, h)),
            pl.BlockSpec((tc, rep * d), lambda h, i: (i, h)),
            pl.BlockSpec((rep, cpb, LANES), lambda h, i: (h, i, 0)),
        ],
        scratch_shapes=[pltpu.VMEM((tc + HALO, d), F32), pltpu.VMEM((tc + HALO, d), F32),
                        pltpu.VMEM((tc + HALO, rep * d), F32)],
        compiler_params=_params(("parallel", "arbitrary")),
        name="gdn_prep",
    )(proj, proj, proj, beta, gc, conv_w_t, conv_w_t, conv_w_t)


def _gdn_rec_kernel(g_ref, n_ref, qp_ref, op_ref, eg_ref, z_ref, on_ref, o_ref, state, obuf):
    hb = g_ref.shape[0]
    tr = qp_ref.shape[0]
    d = GDN_HEAD_DIM

    @pl.when(pl.program_id(1) == 0)
    def _():
        state[...] = jnp.zeros_like(state)

    def chunk_step(c, carry):
        r0 = pl.multiple_of(c * d, d)
        q0 = pl.multiple_of(c * GDN_CHUNK, GDN_CHUNK)
        sbs = [state[j].astype(BF16) for j in range(hb)]
        gs = [_dot(g_ref[j, pl.ds(r0, d), :], sbs[j]) for j in range(hb)]
        os = [_dot(qp_ref[pl.ds(q0, GDN_CHUNK), j * d:(j + 1) * d], sbs[j]) for j in range(hb)]
        for j in range(hb):
            state[j] = eg_ref[j, pl.ds(c, 1), :] * state[j] - gs[j] + n_ref[j, pl.ds(r0, d), :].astype(F32)
            obuf[pl.ds(q0, GDN_CHUNK), j * d:(j + 1) * d] = (
                os[j] + op_ref[pl.ds(q0, GDN_CHUNK), j * d:(j + 1) * d].astype(F32))
        return carry

    lax.fori_loop(0, tr // GDN_CHUNK, chunk_step, 0)

    for j in range(hb):
        cols = slice(j * d, (j + 1) * d)
        o = obuf[:, cols]
        ms = jnp.mean(o * o, axis=-1, keepdims=True)
        o_ref[:, cols] = (o * lax.rsqrt(ms + NORM_EPS) * on_ref[...] * _silu(z_ref[:, cols].astype(F32))
                          ).astype(o_ref.dtype)


def _gdn_recurrence(g_op, n_op, qp, op, eg, proj, z_col0, out_norm, *, hb=8, tr=512):
    n_v, rows, d = g_op.shape
    s = qp.shape[0]
    hb = min(hb, n_v)
    tr = min(tr, s)
    assert n_v % hb == 0 and s % tr == 0 and z_col0 % (hb * d) == 0
    cpb = tr // GDN_CHUNK
    z_blk0 = z_col0 // (hb * d)
    return pl.pallas_call(
        _gdn_rec_kernel,
        out_shape=jax.ShapeDtypeStruct((s, n_v * d), BF16),
        grid=(n_v // hb, s // tr),
        in_specs=[
            pl.BlockSpec((hb, cpb * d, d), lambda h, i: (h, i, 0)),
            pl.BlockSpec((hb, cpb * d, d), lambda h, i: (h, i, 0)),
            pl.BlockSpec((tr, hb * d), lambda h, i: (i, h)),
            pl.BlockSpec((tr, hb * d), lambda h, i: (i, h)),
            pl.BlockSpec((hb, cpb, LANES), lambda h, i: (h, i, 0)),
            pl.BlockSpec((tr, hb * d), lambda h, i: (i, z_blk0 + h)),
            pl.BlockSpec((1, d), lambda h, i: (0, 0)),
        ],
        out_specs=pl.BlockSpec((tr, hb * d), lambda h, i: (i, h)),
        scratch_shapes=[pltpu.VMEM((hb, d, d), F32), pltpu.VMEM((tr, hb * d), F32)],
        compiler_params=_params(("parallel", "arbitrary")),
        name="gdn_recurrence",
    )(g_op, n_op, qp, op, eg, proj, out_norm.reshape(1, d))


def _pad_lanes(vec, offset=0):
    return jnp.zeros((1, LANES), F32).at[0, offset:offset + vec.shape[0]].set(vec.astype(F32))


def kernel(x, norm_mix, norm_mlp, w_mlp_in, w_mlp_out, ab_w_in, ab_conv_w, fox_f_bias, fox_q_norm,
           fox_k_norm, ab_w_out, gdn_w_in, gdn_conv_w, gdn_a_log, gdn_dt_bias, gdn_out_norm, gdn_w_out):
    bsz, s_len, d_model = x.shape
    depth = norm_mix.shape[0]
    width = ab_conv_w.shape[1]
    n_v = gdn_a_log.shape[1]
    qkv = gdn_conv_w.shape[1]
    n_qk = (qkv - n_v * GDN_HEAD_DIM) // (2 * GDN_HEAD_DIM)
    ab_main = 6 * width
    gdn_main = qkv + n_v * GDN_HEAD_DIM
    assert ab_w_in.shape[2] == ab_main + FOX_HEADS and gdn_w_in.shape[2] == gdn_main + 2 * n_v
    assert 2 * n_v <= LANES and FOX_HEADS <= LANES

    def side_weight(w, start):
        cols = w[:, start:]
        return jnp.pad(cols, ((0, 0), (0, LANES - cols.shape[1]))).astype(BF16)

    outs = []
    for b in range(bsz):
        xb = x[b]
        for layer in range(depth):
            i = layer // 2
            if layer % 2 == 0:
                proj, fl = _norm_matmul(xb, norm_mix[layer], ab_w_in[i][:, :ab_main].astype(BF16),
                                        side_w=side_weight(ab_w_in[i], ab_main), name="ab_in_proj")
                y_a, qn, kn, c_col, c_row = _fox_prep(
                    proj, fl, ab_conv_w[i].T, _pad_lanes(fox_f_bias[i]), fox_q_norm[i], fox_k_norm[i], width=width)
                y_b = _fox_attention(qn, kn, proj, 5 * FOX_HEADS, c_col, c_row)
                mixed = jnp.concatenate([y_a, y_b], axis=-1)
                xb = _matmul_res(mixed, ab_w_out[i].astype(BF16), xb, name="ab_out_proj")
            else:
                proj, side = _norm_matmul(xb, norm_mix[layer], gdn_w_in[i][:, :gdn_main].astype(BF16),
                                          side_w=side_weight(gdn_w_in[i], gdn_main), name="gdn_in_proj")
                beta, gc = _gdn_gates(side, _pad_lanes(gdn_a_log[i], n_v), _pad_lanes(gdn_dt_bias[i], n_v))
                g_op, n_op, qp, op, eg = _gdn_prep(proj, beta, gc, gdn_conv_w[i].T, n_qk=n_qk, n_v=n_v)
                o = _gdn_recurrence(g_op, n_op, qp, op, eg, proj, qkv, gdn_out_norm[i])
                xb = _matmul_res(o, gdn_w_out[i].astype(BF16), xb, name="gdn_out_proj")
            hidden = _norm_matmul(xb, norm_mlp[layer], w_mlp_in[layer].astype(BF16), act=True, name="mlp_in")
            xb = _matmul_res(hidden, w_mlp_out[layer].astype(BF16), xb, name="mlp_out")
        outs.append(xb)
    return outs[0][None] if bsz == 1 else jnp.stack(outs, axis=0)
```

```python
import functools

import jax
import jax.numpy as jnp
from jax import lax
from jax.experimental import pallas as pl
from jax.experimental.pallas import tpu as pltpu

F32 = jnp.float32
BF16 = jnp.bfloat16

NORM_EPS = 1e-6
FOX_HEADS = 8
CONV_A_K = 3
GDN_HEAD_DIM = 128
GDN_CONV_K = 4
GDN_CHUNK = 64
LANES = 128
HALO = 8
GROUP = 2 * GDN_CHUNK
INV_BASE = 16
VMEM_LIMIT = 56 * 1024 * 1024
LOG2E = 1.4426950408889634
PRUNE_LOG2 = 160.0
FAST_LOG2 = 100.0
NORM_MARGIN = 1.001


def _params(semantics, vmem=VMEM_LIMIT):
    return pltpu.CompilerParams(dimension_semantics=semantics, vmem_limit_bytes=vmem)


def _dot(a, b):
    return jnp.dot(a, b, preferred_element_type=F32)


def _dot_nt(a, b):
    return lax.dot_general(a, b, (((1,), (1,)), ((), ())), preferred_element_type=F32)


def _sigmoid(x):
    return 1.0 / (1.0 + jnp.exp(-x))


def _silu(x):
    return x * _sigmoid(x)


def _softplus(x):
    return jnp.maximum(x, 0.0) + jnp.log1p(jnp.exp(-jnp.abs(x)))


def _norm_matmul_kernel(*refs, act, has_side):
    if has_side:
        x_ref, g_ref, w_ref, ws_ref, o_ref, side_ref, h_ref = refs
    else:
        x_ref, g_ref, w_ref, o_ref, h_ref = refs

    @pl.when(pl.program_id(1) == 0)
    def _():
        xf = x_ref[...]
        ms = jnp.mean(xf * xf, axis=-1, keepdims=True)
        h = (xf * lax.rsqrt(ms + NORM_EPS) * g_ref[...]).astype(BF16)
        h_ref[...] = h
        if has_side:
            side_ref[...] = _dot(h, ws_ref[...])

    acc = _dot(h_ref[...], w_ref[...].astype(BF16))
    if act:
        acc = jnp.square(jnp.maximum(acc, 0.0))
    o_ref[...] = acc.astype(o_ref.dtype)


def _norm_matmul(x, g, w_stack, layer, n, *, act=False, side_w=None, out_dtype=BF16, tm=1024, tn=1024, name):
    m, k = x.shape
    tm, tn = min(tm, m), min(tn, n)
    assert m % tm == 0 and n % tn == 0 and w_stack.shape[1] == k
    in_specs = [
        pl.BlockSpec((tm, k), lambda i, j: (i, 0)),
        pl.BlockSpec((1, k), lambda i, j: (0, 0)),
        pl.BlockSpec((None, k, tn), lambda i, j: (layer, 0, j)),
    ]
    args = [x, g.reshape(1, k), w_stack]
    out_shape = [jax.ShapeDtypeStruct((m, n), out_dtype)]
    out_specs = [pl.BlockSpec((tm, tn), lambda i, j: (i, j))]
    if side_w is not None:
        in_specs.append(pl.BlockSpec((k, LANES), lambda i, j: (0, 0)))
        args.append(side_w)
        out_shape.append(jax.ShapeDtypeStruct((m, LANES), F32))
        out_specs.append(pl.BlockSpec((tm, LANES), lambda i, j: (i, 0)))
    res = pl.pallas_call(
        functools.partial(_norm_matmul_kernel, act=act, has_side=side_w is not None),
        out_shape=out_shape,
        grid=(m // tm, n // tn),
        in_specs=in_specs,
        out_specs=out_specs,
        scratch_shapes=[pltpu.VMEM((tm, k), BF16)],
        compiler_params=_params(("parallel", "arbitrary")),
        name=name,
    )(*args)
    return res if side_w is not None else res[0]


def _matmul_res_kernel(*refs, n_a):
    a_refs = refs[:n_a]
    w_ref, r_ref, o_ref, wb_ref = refs[n_a:]

    @pl.when(pl.program_id(1) == 0)
    def _():
        wb_ref[...] = w_ref[...].astype(BF16)

    acc = r_ref[...]
    off = 0
    for a_ref in a_refs:
        ka = a_ref.shape[1]
        acc = acc + _dot(a_ref[...], wb_ref[off:off + ka, :])
        off += ka
    o_ref[...] = acc


def _matmul_res(a_list, w_stack, layer, res, *, tm=512, tn=512, name):
    m = res.shape[0]
    k, n = w_stack.shape[1:]
    tm, tn = min(tm, m), min(tn, n)
    assert m % tm == 0 and n % tn == 0 and sum(a.shape[1] for a in a_list) == k
    return pl.pallas_call(
        functools.partial(_matmul_res_kernel, n_a=len(a_list)),
        out_shape=jax.ShapeDtypeStruct((m, n), F32),
        grid=(n // tn, m // tm),
        in_specs=[pl.BlockSpec((tm, a.shape[1]), lambda j, i: (i, 0)) for a in a_list] + [
            pl.BlockSpec((None, k, tn), lambda j, i: (layer, 0, j), pipeline_mode=pl.Buffered(1)),
            pl.BlockSpec((tm, tn), lambda j, i: (i, j)),
        ],
        out_specs=pl.BlockSpec((tm, tn), lambda j, i: (i, j)),
        scratch_shapes=[pltpu.VMEM((k, tn), BF16)],
        compiler_params=_params(("parallel", "arbitrary")),
        name=name,
    )(*a_list, w_stack, res)


def _fox_prep_kernel(gb_ref, gc_ref, xv_ref, q_ref, k_ref, fl_ref, cw_ref, fb_ref, qg_ref, kg_ref,
                     ya_ref, qn_ref, kn_ref, c_ref, ct_ref, qmax_ref, kmax_ref, ubuf, carry, *, head_dim, scale):
    tb = gb_ref.shape[0]
    first = pl.program_id(0) == 0

    @pl.when(first)
    def _():
        ubuf[0:HALO, :] = jnp.zeros((HALO, ubuf.shape[1]), F32)
        carry[...] = jnp.zeros_like(carry)
        qmax_ref[...] = jnp.zeros_like(qmax_ref)
        kmax_ref[...] = jnp.zeros_like(kmax_ref)

    ubuf[HALO:HALO + tb, :] = gc_ref[...].astype(F32) * xv_ref[...].astype(F32)
    y = jnp.zeros((tb, ubuf.shape[1]), F32)
    for j in range(CONV_A_K):
        y = y + cw_ref[j:j + 1, :] * ubuf[pl.ds(HALO - (CONV_A_K - 1) + j, tb), :]
    ya_ref[...] = (gb_ref[...].astype(F32) * y).astype(ya_ref.dtype)
    ubuf[0:HALO, :] = ubuf[tb:tb + HALO, :]

    for h in range(q_ref.shape[1] // head_dim):
        sl = slice(h * head_dim, (h + 1) * head_dim)
        qh = q_ref[:, sl].astype(F32)
        kh = k_ref[:, sl].astype(F32)
        qms = jnp.mean(qh * qh, axis=-1, keepdims=True)
        kms = jnp.mean(kh * kh, axis=-1, keepdims=True)
        qn = (qh * lax.rsqrt(qms + NORM_EPS) * (qg_ref[...] * scale)).astype(qn_ref.dtype)
        kn = (kh * lax.rsqrt(kms + NORM_EPS) * kg_ref[...]).astype(kn_ref.dtype)
        qn_ref[:, sl] = qn
        kn_ref[:, sl] = kn
        qf, kf = qn.astype(F32), kn.astype(F32)
        q_norm = jnp.sqrt(jnp.max(jnp.sum(qf * qf, axis=-1, keepdims=True), axis=0, keepdims=True))
        k_norm = jnp.sqrt(jnp.max(jnp.sum(kf * kf, axis=-1, keepdims=True), axis=0, keepdims=True))
        qmax_ref[h:h + 1, :] = jnp.maximum(qmax_ref[h:h + 1, :], q_norm)
        kmax_ref[h:h + 1, :] = jnp.maximum(kmax_ref[h:h + 1, :], k_norm)

    z = fl_ref[...] + fb_ref[...]
    log_f = jnp.minimum(z, 0.0) - jnp.log1p(jnp.exp(-jnp.abs(z)))
    row = lax.broadcasted_iota(jnp.int32, (tb, tb), 0)
    col = lax.broadcasted_iota(jnp.int32, (tb, tb), 1)
    tri = (col <= row).astype(F32)
    c = jnp.dot(tri, log_f, precision=lax.Precision.HIGHEST, preferred_element_type=F32) + carry[...]
    c_ref[...] = c
    ct_ref[...] = c.T[0:ct_ref.shape[0], :]
    carry[...] = c[tb - 1:tb, :]


def _fox_prep(proj, fl, conv_w_t, f_bias_p, q_norm, k_norm, *, width, tb=512):
    s = proj.shape[0]
    tb = min(tb, s)
    head_dim = width // FOX_HEADS
    col = lambda c: pl.BlockSpec((tb, width), lambda i, c=c: (i, c))
    const = lambda shape: pl.BlockSpec(shape, lambda i: (0, 0))
    return pl.pallas_call(
        functools.partial(_fox_prep_kernel, head_dim=head_dim, scale=LOG2E * head_dim ** -0.5),
        out_shape=[
            jax.ShapeDtypeStruct((s, width), BF16),
            jax.ShapeDtypeStruct((s, width), BF16),
            jax.ShapeDtypeStruct((s, width), BF16),
            jax.ShapeDtypeStruct((s, LANES), F32),
            jax.ShapeDtypeStruct((FOX_HEADS, s), F32),
            jax.ShapeDtypeStruct((FOX_HEADS, LANES), F32),
            jax.ShapeDtypeStruct((FOX_HEADS, LANES), F32),
        ],
        grid=(s // tb,),
        in_specs=[col(0), col(1), col(2), col(3), col(4),
                  pl.BlockSpec((tb, LANES), lambda i: (i, 0)),
                  const((CONV_A_K, width)), const((1, LANES)),
                  const((1, head_dim)), const((1, head_dim))],
        out_specs=[
            pl.BlockSpec((tb, width), lambda i: (i, 0)),
            pl.BlockSpec((tb, width), lambda i: (i, 0)),
            pl.BlockSpec((tb, width), lambda i: (i, 0)),
            pl.BlockSpec((tb, LANES), lambda i: (i, 0)),
            pl.BlockSpec((FOX_HEADS, tb), lambda i: (0, i)),
            const((FOX_HEADS, LANES)), const((FOX_HEADS, LANES)),
        ],
        scratch_shapes=[pltpu.VMEM((tb + HALO, width), F32), pltpu.VMEM((1, LANES), F32)],
        compiler_params=_params(("arbitrary",)),
        name="fox_prep",
    )(proj, proj, proj, proj, proj, fl, conv_w_t, f_bias_p, q_norm.reshape(1, -1), k_norm.reshape(1, -1))


def _fox_attn_kernel(cend_ref, cstart_ref, bound_ref, q_ref, k_ref, v_ref, cq_ref, ck_ref, o_ref,
                     m_ref, l_ref, acc_ref, *, tile):
    h = pl.program_id(0)
    qi = pl.program_id(1)
    reps = tile // LANES
    lane = lax.broadcasted_iota(jnp.int32, (tile, LANES), 1)
    cq = jnp.sum(jnp.where(lane == h, cq_ref[...], 0.0), axis=-1, keepdims=True) * LOG2E
    cq = jnp.broadcast_to(cq, (tile, LANES))
    q = q_ref[...]

    def scores(j):
        start = pl.multiple_of(j * tile, tile)
        ck = ck_ref[h, pl.ds(j, 1), :] * LOG2E
        return _dot_nt(q, k_ref[pl.ds(start, tile), :]) - ck, start

    def accumulate(s, start, stab):
        p = jnp.exp2(s - jnp.concatenate([stab] * reps, axis=1))
        part = p[:, 0:LANES]
        for r in range(1, reps):
            part = part + p[:, r * LANES:(r + 1) * LANES]
        l_ref[...] += part
        acc_ref[...] += _dot(p.astype(v_ref.dtype), v_ref[pl.ds(start, tile), :])

    def online_step(j, masked):
        s, start = scores(j)
        if masked:
            q_pos = lax.broadcasted_iota(jnp.int32, (tile, tile), 0)
            k_pos = lax.broadcasted_iota(jnp.int32, (tile, tile), 1)
            s = jnp.where(k_pos <= q_pos, s, -jnp.inf)
        m_old = m_ref[...]
        m_new = jnp.maximum(m_old, jnp.broadcast_to(jnp.max(s, axis=-1, keepdims=True), (tile, LANES)) + cq)
        alpha = jnp.exp2(m_old - m_new)
        l_ref[...] = alpha * l_ref[...]
        acc_ref[...] = alpha * acc_ref[...]
        m_ref[...] = m_new
        accumulate(s, start, m_new - cq)

    m_ref[...] = jnp.full_like(m_ref, -jnp.inf)
    l_ref[...] = jnp.zeros_like(l_ref)
    acc_ref[...] = jnp.zeros_like(acc_ref)
    online_step(qi, True)

    two_b = 2.0 * bound_ref[h]
    reach = two_b + PRUNE_LOG2
    c_tile = cstart_ref[h, qi]

    def in_reach(j):
        return jnp.logical_and(j >= 0, cend_ref[h, jnp.maximum(j, 0)] - c_tile <= reach)

    def fast_block(j):
        s, start = scores(j)
        accumulate(s, start, m_ref[...] - cq)
        return j - 1

    def online_block(j):
        online_step(j, False)
        return j - 1

    @pl.when(two_b <= FAST_LOG2)
    def _():
        lax.while_loop(in_reach, fast_block, qi - 1)

    @pl.when(two_b > FAST_LOG2)
    def _():
        lax.while_loop(in_reach, online_block, qi - 1)

    o_ref[...] = (acc_ref[...] / jnp.sum(l_ref[...], axis=-1, keepdims=True)).astype(o_ref.dtype)


def _fox_attention(qn, kn, proj, v_col0, c_col, c_row, qmax, kmax, *, tile=512):
    s, width = qn.shape
    d = width // FOX_HEADS
    tile = min(tile, s)
    assert d == LANES and s % tile == 0 and tile % LANES == 0
    n_blk = s // tile
    cend = c_row[:, tile - 1::tile] * LOG2E
    cstart = c_row[:, ::tile] * LOG2E
    bound = qmax[:, 0] * kmax[:, 0] * NORM_MARGIN
    smem = pl.BlockSpec(memory_space=pltpu.SMEM)
    return pl.pallas_call(
        functools.partial(_fox_attn_kernel, tile=tile),
        out_shape=jax.ShapeDtypeStruct((s, width), BF16),
        grid=(FOX_HEADS, n_blk),
        in_specs=[
            smem, smem, smem,
            pl.BlockSpec((tile, d), lambda h, qi: (qi, h)),
            pl.BlockSpec((s, d), lambda h, qi: (0, h)),
            pl.BlockSpec((s, d), lambda h, qi: (0, v_col0 + h)),
            pl.BlockSpec((tile, LANES), lambda h, qi: (qi, 0)),
            pl.BlockSpec((FOX_HEADS, n_blk, tile), lambda h, qi: (0, 0, 0)),
        ],
        out_specs=pl.BlockSpec((tile, d), lambda h, qi: (qi, h)),
        scratch_shapes=[pltpu.VMEM((tile, LANES), F32), pltpu.VMEM((tile, LANES), F32),
                        pltpu.VMEM((tile, d), F32)],
        compiler_params=_params(("parallel", "arbitrary")),
        name="fox_attention",
    )(cend, cstart, bound, qn, kn, proj, c_col, c_row.reshape(FOX_HEADS, n_blk, tile))


def _gdn_gates_kernel(side_ref, alog_ref, dtb_ref, beta_ref, gc_ref):
    side = side_ref[...]
    tb = side.shape[0]
    beta_ref[...] = _sigmoid(side)
    g = -jnp.exp(alog_ref[...]) * _softplus(side + dtb_ref[...])
    row = lax.broadcasted_iota(jnp.int32, (tb, tb), 0)
    col = lax.broadcasted_iota(jnp.int32, (tb, tb), 1)
    tri = jnp.logical_and(row // GDN_CHUNK == col // GDN_CHUNK, col <= row).astype(F32)
    gc_ref[...] = jnp.dot(tri, g, precision=lax.Precision.HIGHEST, preferred_element_type=F32)


def _gdn_gates(side, alog_p, dtb_p, *, tb=512):
    s = side.shape[0]
    tb = min(tb, s)
    assert s % tb == 0 and tb % GDN_CHUNK == 0
    blk = pl.BlockSpec((tb, LANES), lambda i: (i, 0))
    const = pl.BlockSpec((1, LANES), lambda i: (0, 0))
    return pl.pallas_call(
        _gdn_gates_kernel,
        out_shape=[jax.ShapeDtypeStruct((s, LANES), F32), jax.ShapeDtypeStruct((s, LANES), F32)],
        grid=(s // tb,),
        in_specs=[blk, const, const],
        out_specs=[blk, blk],
        compiler_params=_params(("parallel",)),
        name="gdn_gates",
    )(side, alog_p, dtb_p)


def _conv_silu(buf, raw_ref, w_ref, tc):
    buf[HALO:HALO + tc, :] = raw_ref[...].astype(F32)
    y = jnp.zeros((tc, buf.shape[1]), F32)
    for j in range(GDN_CONV_K):
        y = y + w_ref[j:j + 1, :] * buf[pl.ds(HALO - (GDN_CONV_K - 1) + j, tc), :]
    buf[0:HALO, :] = buf[tc:tc + HALO, :]
    return _silu(y)


def _l2_norm(x):
    return x * lax.rsqrt(jnp.sum(x * x, axis=-1, keepdims=True) + NORM_EPS)


def _unit_lower_inverses(l_mats, eye, row, col):
    n = eye.shape[0]
    same_base = (row // INV_BASE) == (col // INV_BASE)
    lbs = [jnp.where(same_base, l, 0.0).astype(BF16) for l in l_mats]
    ps = [eye - lb.astype(F32) for lb in lbs]
    ms = [_dot(lb, lb) for lb in lbs]
    power = 2
    while power < INV_BASE:
        mbs = [m.astype(BF16) for m in ms]
        if 2 * power < INV_BASE:
            boths = [_dot(mb, jnp.concatenate([mb, p.astype(BF16)], axis=1)) for mb, p in zip(mbs, ps)]
            ms = [b[:, :n] for b in boths]
            xs = [b[:, n:] for b in boths]
        else:
            xs = [_dot(mb, p.astype(BF16)) for mb, p in zip(mbs, ps)]
        ps = [p + x for p, x in zip(ps, xs)]
        power *= 2
    size = INV_BASE
    while size < GDN_CHUNK:
        below = jnp.logical_and(row // (2 * size) == col // (2 * size), row // size == col // size + 1)
        c_offs = [jnp.where(below, l, 0.0).astype(BF16) for l in l_mats]
        pbs = [p.astype(BF16) for p in ps]
        ts = [_dot(c, pb).astype(BF16) for c, pb in zip(c_offs, pbs)]
        ps = [p - _dot(pb, t) for p, pb, t in zip(ps, pbs, ts)]
        size *= 2
    return ps


def _gdn_prep_kernel(q_ref, k_ref, v_ref, beta_ref, gc_ref, wq_ref, wk_ref, wv_ref,
                     g_out, n_out, qp_out, op_out, eg_out, qbuf, kbuf, vbuf, *, n_vheads):
    tc = q_ref.shape[0]
    d = GDN_HEAD_DIM
    hq = pl.program_id(0)
    rep = v_ref.shape[1] // d
    chunks_per_group = GROUP // GDN_CHUNK

    @pl.when(pl.program_id(1) == 0)
    def _():
        qbuf[0:HALO, :] = jnp.zeros((HALO, qbuf.shape[1]), F32)
        kbuf[0:HALO, :] = jnp.zeros((HALO, kbuf.shape[1]), F32)
        vbuf[0:HALO, :] = jnp.zeros((HALO, vbuf.shape[1]), F32)

    q = _l2_norm(_conv_silu(qbuf, q_ref, wq_ref, tc)) * (d ** -0.5)
    k = _l2_norm(_conv_silu(kbuf, k_ref, wk_ref, tc))
    v = _conv_silu(vbuf, v_ref, wv_ref, tc)

    beta_all = beta_ref[...]
    gc_all = gc_ref[...]
    lane = lax.broadcasted_iota(jnp.int32, beta_all.shape, 1)
    row = lax.broadcasted_iota(jnp.int32, (GROUP, GROUP), 0)
    col = lax.broadcasted_iota(jnp.int32, (GROUP, GROUP), 1)
    same_chunk = (row // GDN_CHUNK) == (col // GDN_CHUNK)
    incl = jnp.logical_and(same_chunk, col <= row)
    strict = jnp.logical_and(same_chunk, col < row)
    eye = (row == col).astype(F32)

    heads = []
    for j in range(rep):
        vh = hq * rep + j
        beta = jnp.sum(jnp.where(lane == vh, beta_all, 0.0), axis=-1, keepdims=True)
        gc = jnp.sum(jnp.where(lane == n_vheads + vh, gc_all, 0.0), axis=-1, keepdims=True)
        heads.append((beta, gc))

    probs = [(j, r) for j in range(rep) for r in range(tc // GROUP)]
    rows_of = lambda r: slice(r * GROUP, (r + 1) * GROUP)
    kgs = [k[rows_of(r)] for _, r in probs]
    qgs = [q[rows_of(r)] for _, r in probs]
    kgbs = [kg.astype(BF16) for kg in kgs]
    bgs = [heads[j][0][rows_of(r)] for j, r in probs]
    gcbs = [jnp.broadcast_to(heads[j][1][rows_of(r)], (GROUP, GROUP)) for j, r in probs]
    gcrs = [gcb.T for gcb in gcbs]
    decays = [jnp.where(incl, jnp.exp(jnp.where(incl, gcb - gcr, 0.0)), 0.0) for gcb, gcr in zip(gcbs, gcrs)]
    kbs = [kg * bg for kg, bg in zip(kgs, bgs)]
    kqs = [_dot_nt(jnp.concatenate([kb, qg], axis=0).astype(BF16), kgb) for kb, qg, kgb in zip(kbs, qgs, kgbs)]
    l_mats = [jnp.where(strict, kq[:GROUP] * dec, 0.0) for kq, dec in zip(kqs, decays)]
    scores = [(kq[GROUP:] * dec).astype(BF16) for kq, dec in zip(kqs, decays)]
    t_invs = _unit_lower_inverses(l_mats, eye, row, col)
    egcs = [jnp.exp(gcb) for gcb in gcbs]
    rhs = [jnp.concatenate([(v[rows_of(r), j * d:(j + 1) * d] * bg).astype(BF16), (kb * egc).astype(BF16)], axis=1)
           for (j, r), bg, kb, egc in zip(probs, bgs, kbs, egcs)]
    uws = [_dot(t.astype(BF16), x) for t, x in zip(t_invs, rhs)]
    wus = [jnp.concatenate([uw[:, d:], uw[:, :d]], axis=1).astype(BF16) for uw in uws]
    g_lasts = [jnp.where(row < GDN_CHUNK, gcb[GDN_CHUNK - 1:GDN_CHUNK, :], gcb[GROUP - 1:GROUP, :]) for gcb in gcbs]
    kdts = [(kg * jnp.exp(gl - gcb)).T for kg, gl, gcb in zip(kgs, g_lasts, gcbs)]
    qos = [_dot(sc, wu) for sc, wu in zip(scores, wus)]
    gns = [[_dot(jnp.where(col // GDN_CHUNK == c, kdt, 0.0).astype(BF16), wu) for c in range(chunks_per_group)]
           for kdt, wu in zip(kdts, wus)]

    for p, (j, r) in enumerate(probs):
        cols = slice(j * d, (j + 1) * d)
        qp_out[rows_of(r), cols] = (qgs[p] * egcs[p] - qos[p][:, :d]).astype(qp_out.dtype)
        op_out[rows_of(r), cols] = qos[p][:, d:].astype(op_out.dtype)
        for c in range(chunks_per_group):
            chunk = r * chunks_per_group + c
            g_out[j, chunk * d:(chunk + 1) * d, :] = gns[p][c][:, :d].astype(g_out.dtype)
            n_out[j, chunk * d:(chunk + 1) * d, :] = gns[p][c][:, d:].astype(n_out.dtype)
            last = (c + 1) * GDN_CHUNK - 1
            eg_out[j, chunk:chunk + 1, :] = jnp.exp(gcbs[p][last:last + 1, :])


def _gdn_prep(proj, beta, gc, conv_w_t, *, n_qk, n_v, tc=512):
    s = proj.shape[0]
    d = GDN_HEAD_DIM
    rep = n_v // n_qk
    tc = min(tc, s)
    assert s % tc == 0 and tc % GROUP == 0
    kw = n_qk * d
    v_blk0 = 2 * kw // (rep * d)
    n_chunks = s // GDN_CHUNK
    cpb = tc // GDN_CHUNK
    return pl.pallas_call(
        functools.partial(_gdn_prep_kernel, n_vheads=n_v),
        out_shape=[
            jax.ShapeDtypeStruct((n_v, n_chunks * d, d), BF16),
            jax.ShapeDtypeStruct((n_v, n_chunks * d, d), BF16),
            jax.ShapeDtypeStruct((s, n_v * d), BF16),
            jax.ShapeDtypeStruct((s, n_v * d), BF16),
            jax.ShapeDtypeStruct((n_v, n_chunks, LANES), F32),
        ],
        grid=(n_qk, s // tc),
        in_specs=[
            pl.BlockSpec((tc, d), lambda h, i: (i, h)),
            pl.BlockSpec((tc, d), lambda h, i: (i, n_qk + h)),
            pl.BlockSpec((tc, rep * d), lambda h, i: (i, v_blk0 + h)),
            pl.BlockSpec((tc, LANES), lambda h, i: (i, 0)),
            pl.BlockSpec((tc, LANES), lambda h, i: (i, 0)),
            pl.BlockSpec((GDN_CONV_K, d), lambda h, i: (0, h)),
            pl.BlockSpec((GDN_CONV_K, d), lambda h, i: (0, n_qk + h)),
            pl.BlockSpec((GDN_CONV_K, rep * d), lambda h, i: (0, v_blk0 + h)),
        ],
        out_specs=[
            pl.BlockSpec((rep, cpb * d, d), lambda h, i: (h, i, 0)),
            pl.BlockSpec((rep, cpb * d, d), lambda h, i: (h, i, 0)),
            pl.BlockSpec((tc, rep * d), lambda h, i: (i, h)),
            pl.BlockSpec((tc, rep * d), lambda h, i: (i, h)),
            pl.BlockSpec((rep, cpb, LANES), lambda h, i: (h, i, 0)),
        ],
        scratch_shapes=[pltpu.VMEM((tc + HALO, d), F32), pltpu.VMEM((tc + HALO, d), F32),
                        pltpu.VMEM((tc + HALO, rep * d), F32)],
        compiler_params=_params(("parallel", "arbitrary")),
        name="gdn_prep",
    )(proj, proj, proj, beta, gc, conv_w_t, conv_w_t, conv_w_t)


def _gdn_rec_kernel(g_ref, n_ref, qp_ref, op_ref, eg_ref, z_ref, on_ref, o_ref, state, obuf):
    hb = g_ref.shape[0]
    tr = qp_ref.shape[0]
    d = GDN_HEAD_DIM

    @pl.when(pl.program_id(1) == 0)
    def _():
        state[...] = jnp.zeros_like(state)

    def chunk_step(c, carry):
        r0 = pl.multiple_of(c * d, d)
        q0 = pl.multiple_of(c * GDN_CHUNK, GDN_CHUNK)
        sbs = [state[j].astype(BF16) for j in range(hb)]
        gs = [_dot(g_ref[j, pl.ds(r0, d), :], sbs[j]) for j in range(hb)]
        os = [_dot(qp_ref[pl.ds(q0, GDN_CHUNK), j * d:(j + 1) * d], sbs[j]) for j in range(hb)]
        for j in range(hb):
            state[j] = eg_ref[j, pl.ds(c, 1), :] * state[j] - gs[j] + n_ref[j, pl.ds(r0, d), :].astype(F32)
            obuf[pl.ds(q0, GDN_CHUNK), j * d:(j + 1) * d] = (
                os[j] + op_ref[pl.ds(q0, GDN_CHUNK), j * d:(j + 1) * d].astype(F32))
        return carry

    lax.fori_loop(0, tr // GDN_CHUNK, chunk_step, 0)

    for j in range(hb):
        cols = slice(j * d, (j + 1) * d)
        o = obuf[:, cols]
        ms = jnp.mean(o * o, axis=-1, keepdims=True)
        o_ref[:, cols] = (o * lax.rsqrt(ms + NORM_EPS) * on_ref[...] * _silu(z_ref[:, cols].astype(F32))
                          ).astype(o_ref.dtype)


def _gdn_recurrence(g_op, n_op, qp, op, eg, proj, z_col0, out_norm, *, hb=8, tr=512):
    n_v, rows, d = g_op.shape
    s = qp.shape[0]
    hb = min(hb, n_v)
    tr = min(tr, s)
    assert n_v % hb == 0 and s % tr == 0 and z_col0 % (hb * d) == 0
    cpb = tr // GDN_CHUNK
    z_blk0 = z_col0 // (hb * d)
    return pl.pallas_call(
        _gdn_rec_kernel,
        out_shape=jax.ShapeDtypeStruct((s, n_v * d), BF16),
        grid=(n_v // hb, s // tr),
        in_specs=[
            pl.BlockSpec((hb, cpb * d, d), lambda h, i: (h, i, 0)),
            pl.BlockSpec((hb, cpb * d, d), lambda h, i: (h, i, 0)),
            pl.BlockSpec((tr, hb * d), lambda h, i: (i, h)),
            pl.BlockSpec((tr, hb * d), lambda h, i: (i, h)),
            pl.BlockSpec((hb, cpb, LANES), lambda h, i: (h, i, 0)),
            pl.BlockSpec((tr, hb * d), lambda h, i: (i, z_blk0 + h)),
            pl.BlockSpec((1, d), lambda h, i: (0, 0)),
        ],
        out_specs=pl.BlockSpec((tr, hb * d), lambda h, i: (i, h)),
        scratch_shapes=[pltpu.VMEM((hb, d, d), F32), pltpu.VMEM((tr, hb * d), F32)],
        compiler_params=_params(("parallel", "arbitrary")),
        name="gdn_recurrence",
    )(g_op, n_op, qp, op, eg, proj, out_norm.reshape(1, d))


def _pad_lanes(vec, offset=0):
    return jnp.zeros((1, LANES), F32).at[0, offset:offset + vec.shape[0]].set(vec.astype(F32))


def kernel(x, norm_mix, norm_mlp, w_mlp_in, w_mlp_out, ab_w_in, ab_conv_w, fox_f_bias, fox_q_norm,
           fox_k_norm, ab_w_out, gdn_w_in, gdn_conv_w, gdn_a_log, gdn_dt_bias, gdn_out_norm, gdn_w_out):
    bsz, s_len, d_model = x.shape
    depth = norm_mix.shape[0]
    width = ab_conv_w.shape[1]
    n_v = gdn_a_log.shape[1]
    qkv = gdn_conv_w.shape[1]
    n_qk = (qkv - n_v * GDN_HEAD_DIM) // (2 * GDN_HEAD_DIM)
    ab_main = 6 * width
    gdn_main = qkv + n_v * GDN_HEAD_DIM
    d_ff = w_mlp_in.shape[2]
    assert ab_w_in.shape[2] == ab_main + FOX_HEADS and gdn_w_in.shape[2] == gdn_main + 2 * n_v
    assert 2 * n_v <= LANES and FOX_HEADS <= LANES

    def side_weight(w, start):
        cols = w[:, start:]
        return jnp.pad(cols, ((0, 0), (0, LANES - cols.shape[1]))).astype(BF16)

    outs = []
    for b in range(bsz):
        xb = x[b]
        for layer in range(depth):
            i = layer // 2
            if layer % 2 == 0:
                proj, fl = _norm_matmul(xb, norm_mix[layer], ab_w_in, i, ab_main,
                                        side_w=side_weight(ab_w_in[i], ab_main), name="ab_in_proj")
                y_a, qn, kn, c_col, c_row, qmax, kmax = _fox_prep(
                    proj, fl, ab_conv_w[i].T, _pad_lanes(fox_f_bias[i]), fox_q_norm[i], fox_k_norm[i], width=width)
                y_b = _fox_attention(qn, kn, proj, 5 * FOX_HEADS, c_col, c_row, qmax, kmax)
                xb = _matmul_res([y_a, y_b], ab_w_out, i, xb, name="ab_out_proj")
            else:
                proj, side = _norm_matmul(xb, norm_mix[layer], gdn_w_in, i, gdn_main,
                                          side_w=side_weight(gdn_w_in[i], gdn_main), name="gdn_in_proj")
                beta, gc = _gdn_gates(side, _pad_lanes(gdn_a_log[i], n_v), _pad_lanes(gdn_dt_bias[i], n_v))
                g_op, n_op, qp, op, eg = _gdn_prep(proj, beta, gc, gdn_conv_w[i].T, n_qk=n_qk, n_v=n_v)
                o = _gdn_recurrence(g_op, n_op, qp, op, eg, proj, qkv, gdn_out_norm[i])
                xb = _matmul_res([o], gdn_w_out, i, xb, name="gdn_out_proj")
            hidden = _norm_matmul(xb, norm_mlp[layer], w_mlp_in, layer, d_ff, act=True, name="mlp_in")
            xb = _matmul_res([hidden], w_mlp_out, layer, xb, name="mlp_out")
        outs.append(xb)
    return outs[0][None] if bsz == 1 else jnp.stack(outs, axis=0)
```

```python
import functools

import jax
import jax.numpy as jnp
from jax import lax
from jax.experimental import pallas as pl
from jax.experimental.pallas import tpu as pltpu

F32 = jnp.float32
BF16 = jnp.bfloat16

NORM_EPS = 1e-6
FOX_HEADS = 8
CONV_A_K = 3
GDN_HEAD_DIM = 128
GDN_CONV_K = 4
GDN_CHUNK = 64
LANES = 128
HALO = 8
GROUP = 2 * GDN_CHUNK
INV_BASE = 16
VMEM_LIMIT = 56 * 1024 * 1024
LOG2E = 1.4426950408889634
PRUNE_LOG2 = 160.0
FAST_LOG2 = 100.0
NORM_MARGIN = 1.001


def _params(semantics, vmem=VMEM_LIMIT):
    return pltpu.CompilerParams(dimension_semantics=semantics, vmem_limit_bytes=vmem)


def _dot(a, b):
    return jnp.dot(a, b, preferred_element_type=F32)


def _dot_nt(a, b):
    return lax.dot_general(a, b, (((1,), (1,)), ((), ())), preferred_element_type=F32)


def _sigmoid(x):
    return 0.5 * jnp.tanh(0.5 * x) + 0.5


def _silu(x):
    return x * _sigmoid(x)


def _softplus(x):
    return jnp.maximum(x, 0.0) + jnp.log1p(jnp.exp(-jnp.abs(x)))


def _norm_matmul_kernel(*refs, act, has_side):
    if has_side:
        x_ref, g_ref, w_ref, ws_ref, o_ref, side_ref, h_ref = refs
    else:
        x_ref, g_ref, w_ref, o_ref, h_ref = refs

    @pl.when(pl.program_id(1) == 0)
    def _():
        xf = x_ref[...]
        ms = jnp.mean(xf * xf, axis=-1, keepdims=True)
        h = (xf * lax.rsqrt(ms + NORM_EPS) * g_ref[...]).astype(BF16)
        h_ref[...] = h
        if has_side:
            side_ref[...] = _dot(h, ws_ref[...].astype(BF16))

    acc = _dot(h_ref[...], w_ref[...].astype(BF16))
    if act:
        acc = jnp.square(jnp.maximum(acc, 0.0))
    o_ref[...] = acc.astype(o_ref.dtype)


def _norm_matmul(x, g, w_stack, layer, n, *, act=False, side_w=None, out_dtype=BF16, tm=1024, tn=1024, name):
    m, k = x.shape
    tm, tn = min(tm, m), min(tn, n)
    assert m % tm == 0 and n % tn == 0 and w_stack.shape[1] == k
    in_specs = [
        pl.BlockSpec((tm, k), lambda i, j: (i, 0)),
        pl.BlockSpec((1, k), lambda i, j: (0, 0)),
        pl.BlockSpec((None, k, tn), lambda i, j: (layer, 0, j)),
    ]
    args = [x, g.reshape(1, k), w_stack]
    out_shape = [jax.ShapeDtypeStruct((m, n), out_dtype)]
    out_specs = [pl.BlockSpec((tm, tn), lambda i, j: (i, j))]
    if side_w is not None:
        in_specs.append(pl.BlockSpec((k, LANES), lambda i, j: (0, 0)))
        args.append(side_w)
        out_shape.append(jax.ShapeDtypeStruct((m, LANES), F32))
        out_specs.append(pl.BlockSpec((tm, LANES), lambda i, j: (i, 0)))
    res = pl.pallas_call(
        functools.partial(_norm_matmul_kernel, act=act, has_side=side_w is not None),
        out_shape=out_shape,
        grid=(m // tm, n // tn),
        in_specs=in_specs,
        out_specs=out_specs,
        scratch_shapes=[pltpu.VMEM((tm, k), BF16)],
        compiler_params=_params(("parallel", "arbitrary")),
        name=name,
    )(*args)
    return res if side_w is not None else res[0]


def _matmul_res_kernel(*refs, n_a):
    a_refs = refs[:n_a]
    w_ref, r_ref, o_ref, wb_ref = refs[n_a:]

    @pl.when(pl.program_id(1) == 0)
    def _():
        wb_ref[...] = w_ref[...].astype(BF16)

    acc = r_ref[...]
    off = 0
    for a_ref in a_refs:
        ka = a_ref.shape[1]
        acc = acc + _dot(a_ref[...], wb_ref[off:off + ka, :])
        off += ka
    o_ref[...] = acc


RESIDENT_W_BYTES = 24 * 1024 * 1024


def _matmul_res(a_list, w_stack, layer, res, *, tm=512, name):
    m = res.shape[0]
    k, n = w_stack.shape[1:]
    tn = n
    while k * tn * 6 > RESIDENT_W_BYTES and tn % 2 == 0:
        tn //= 2
    tm = min(tm, m)
    assert m % tm == 0 and n % tn == 0 and sum(a.shape[1] for a in a_list) == k
    return pl.pallas_call(
        functools.partial(_matmul_res_kernel, n_a=len(a_list)),
        out_shape=jax.ShapeDtypeStruct((m, n), F32),
        grid=(n // tn, m // tm),
        in_specs=[pl.BlockSpec((tm, a.shape[1]), lambda j, i: (i, 0)) for a in a_list] + [
            pl.BlockSpec((None, k, tn), lambda j, i: (layer, 0, j), pipeline_mode=pl.Buffered(1)),
            pl.BlockSpec((tm, tn), lambda j, i: (i, j)),
        ],
        out_specs=pl.BlockSpec((tm, tn), lambda j, i: (i, j)),
        scratch_shapes=[pltpu.VMEM((k, tn), BF16)],
        compiler_params=_params(("parallel", "arbitrary")),
        name=name,
    )(*a_list, w_stack, res)


def _fox_prep_kernel(gb_ref, gc_ref, xv_ref, q_ref, k_ref, fl_ref, cw_ref, fb_ref, qg_ref, kg_ref,
                     ya_ref, qn_ref, kn_ref, c_ref, ct_ref, qmax_ref, kmax_ref, ubuf, carry, *, head_dim, scale):
    tb = gb_ref.shape[0]
    first = pl.program_id(0) == 0

    @pl.when(first)
    def _():
        ubuf[0:HALO, :] = jnp.zeros((HALO, ubuf.shape[1]), F32)
        carry[...] = jnp.zeros_like(carry)
        qmax_ref[...] = jnp.zeros_like(qmax_ref)
        kmax_ref[...] = jnp.zeros_like(kmax_ref)

    ubuf[HALO:HALO + tb, :] = gc_ref[...].astype(F32) * xv_ref[...].astype(F32)
    y = jnp.zeros((tb, ubuf.shape[1]), F32)
    for j in range(CONV_A_K):
        y = y + cw_ref[j:j + 1, :] * ubuf[pl.ds(HALO - (CONV_A_K - 1) + j, tb), :]
    ya_ref[...] = (gb_ref[...].astype(F32) * y).astype(ya_ref.dtype)
    ubuf[0:HALO, :] = ubuf[tb:tb + HALO, :]

    for h in range(q_ref.shape[1] // head_dim):
        sl = slice(h * head_dim, (h + 1) * head_dim)
        qh = q_ref[:, sl].astype(F32)
        kh = k_ref[:, sl].astype(F32)
        qms = jnp.mean(qh * qh, axis=-1, keepdims=True)
        kms = jnp.mean(kh * kh, axis=-1, keepdims=True)
        qn = (qh * lax.rsqrt(qms + NORM_EPS) * (qg_ref[...] * scale)).astype(qn_ref.dtype)
        kn = (kh * lax.rsqrt(kms + NORM_EPS) * kg_ref[...]).astype(kn_ref.dtype)
        qn_ref[:, sl] = qn
        kn_ref[:, sl] = kn
        qf, kf = qn.astype(F32), kn.astype(F32)
        q_norm = jnp.sqrt(jnp.max(jnp.sum(qf * qf, axis=-1, keepdims=True), axis=0, keepdims=True))
        k_norm = jnp.sqrt(jnp.max(jnp.sum(kf * kf, axis=-1, keepdims=True), axis=0, keepdims=True))
        qmax_ref[h:h + 1, :] = jnp.maximum(qmax_ref[h:h + 1, :], q_norm)
        kmax_ref[h:h + 1, :] = jnp.maximum(kmax_ref[h:h + 1, :], k_norm)

    z = fl_ref[...] + fb_ref[...]
    log_f = jnp.minimum(z, 0.0) - jnp.log1p(jnp.exp(-jnp.abs(z)))
    row = lax.broadcasted_iota(jnp.int32, (tb, tb), 0)
    col = lax.broadcasted_iota(jnp.int32, (tb, tb), 1)
    tri = (col <= row).astype(F32)
    c = jnp.dot(tri, log_f, precision=lax.Precision.HIGHEST, preferred_element_type=F32) + carry[...]
    c_ref[...] = c
    ct_ref[...] = c.T[0:ct_ref.shape[0], :]
    carry[...] = c[tb - 1:tb, :]


def _fox_prep(proj, fl, conv_w_t, f_bias_p, q_norm, k_norm, *, width, tb=512):
    s = proj.shape[0]
    tb = min(tb, s)
    head_dim = width // FOX_HEADS
    col = lambda c: pl.BlockSpec((tb, width), lambda i, c=c: (i, c))
    const = lambda shape: pl.BlockSpec(shape, lambda i: (0, 0))
    return pl.pallas_call(
        functools.partial(_fox_prep_kernel, head_dim=head_dim, scale=LOG2E * head_dim ** -0.5),
        out_shape=[
            jax.ShapeDtypeStruct((s, width), BF16),
            jax.ShapeDtypeStruct((s, width), BF16),
            jax.ShapeDtypeStruct((s, width), BF16),
            jax.ShapeDtypeStruct((s, LANES), F32),
            jax.ShapeDtypeStruct((FOX_HEADS, s), F32),
            jax.ShapeDtypeStruct((FOX_HEADS, LANES), F32),
            jax.ShapeDtypeStruct((FOX_HEADS, LANES), F32),
        ],
        grid=(s // tb,),
        in_specs=[col(0), col(1), col(2), col(3), col(4),
                  pl.BlockSpec((tb, LANES), lambda i: (i, 0)),
                  const((CONV_A_K, width)), const((1, LANES)),
                  const((1, head_dim)), const((1, head_dim))],
        out_specs=[
            pl.BlockSpec((tb, width), lambda i: (i, 0)),
            pl.BlockSpec((tb, width), lambda i: (i, 0)),
            pl.BlockSpec((tb, width), lambda i: (i, 0)),
            pl.BlockSpec((tb, LANES), lambda i: (i, 0)),
            pl.BlockSpec((FOX_HEADS, tb), lambda i: (0, i)),
            const((FOX_HEADS, LANES)), const((FOX_HEADS, LANES)),
        ],
        scratch_shapes=[pltpu.VMEM((tb + HALO, width), F32), pltpu.VMEM((1, LANES), F32)],
        compiler_params=_params(("arbitrary",)),
        name="fox_prep",
    )(proj, proj, proj, proj, proj, fl, conv_w_t, f_bias_p, q_norm.reshape(1, -1), k_norm.reshape(1, -1))


def _fox_attn_kernel(cend_ref, cstart_ref, bound_ref, q_ref, k_ref, v_ref, cq_ref, ck_ref, o_ref,
                     m_ref, l_ref, acc_ref, *, tile):
    h = pl.program_id(0)
    qi = pl.program_id(1)
    reps = tile // LANES
    lane = lax.broadcasted_iota(jnp.int32, (tile, LANES), 1)
    cq = jnp.sum(jnp.where(lane == h, cq_ref[...], 0.0), axis=-1, keepdims=True) * LOG2E
    cq = jnp.broadcast_to(cq, (tile, LANES))
    q = q_ref[...]

    def scores(j):
        start = pl.multiple_of(j * tile, tile)
        ck = ck_ref[h, pl.ds(j, 1), :] * LOG2E
        return _dot_nt(q, k_ref[pl.ds(start, tile), :]) - ck, start

    def accumulate(s, start, stab):
        p = jnp.exp2(s - jnp.concatenate([stab] * reps, axis=1))
        part = p[:, 0:LANES]
        for r in range(1, reps):
            part = part + p[:, r * LANES:(r + 1) * LANES]
        l_ref[...] += part
        acc_ref[...] += _dot(p.astype(v_ref.dtype), v_ref[pl.ds(start, tile), :])

    def online_step(j, masked):
        s, start = scores(j)
        if masked:
            q_pos = lax.broadcasted_iota(jnp.int32, (tile, tile), 0)
            k_pos = lax.broadcasted_iota(jnp.int32, (tile, tile), 1)
            s = jnp.where(k_pos <= q_pos, s, -jnp.inf)
        m_old = m_ref[...]
        m_new = jnp.maximum(m_old, jnp.broadcast_to(jnp.max(s, axis=-1, keepdims=True), (tile, LANES)) + cq)
        alpha = jnp.exp2(m_old - m_new)
        l_ref[...] = alpha * l_ref[...]
        acc_ref[...] = alpha * acc_ref[...]
        m_ref[...] = m_new
        accumulate(s, start, m_new - cq)

    m_ref[...] = jnp.full_like(m_ref, -jnp.inf)
    l_ref[...] = jnp.zeros_like(l_ref)
    acc_ref[...] = jnp.zeros_like(acc_ref)
    online_step(qi, True)

    two_b = 2.0 * bound_ref[h]
    reach = two_b + PRUNE_LOG2
    c_tile = cstart_ref[h, qi]

    def in_reach(j):
        return jnp.logical_and(j >= 0, cend_ref[h, jnp.maximum(j, 0)] - c_tile <= reach)

    def fast_block(j):
        s, start = scores(j)
        accumulate(s, start, m_ref[...] - cq)
        return j - 1

    def online_block(j):
        online_step(j, False)
        return j - 1

    @pl.when(two_b <= FAST_LOG2)
    def _():
        lax.while_loop(in_reach, fast_block, qi - 1)

    @pl.when(two_b > FAST_LOG2)
    def _():
        lax.while_loop(in_reach, online_block, qi - 1)

    o_ref[...] = (acc_ref[...] / jnp.sum(l_ref[...], axis=-1, keepdims=True)).astype(o_ref.dtype)


def _fox_attention(qn, kn, proj, v_col0, c_col, c_row, qmax, kmax, *, tile=512):
    s, width = qn.shape
    d = width // FOX_HEADS
    tile = min(tile, s)
    assert d == LANES and s % tile == 0 and tile % LANES == 0
    n_blk = s // tile
    cend = c_row[:, tile - 1::tile] * LOG2E
    cstart = c_row[:, ::tile] * LOG2E
    bound = qmax[:, 0] * kmax[:, 0] * NORM_MARGIN
    smem = pl.BlockSpec(memory_space=pltpu.SMEM)
    return pl.pallas_call(
        functools.partial(_fox_attn_kernel, tile=tile),
        out_shape=jax.ShapeDtypeStruct((s, width), BF16),
        grid=(FOX_HEADS, n_blk),
        in_specs=[
            smem, smem, smem,
            pl.BlockSpec((tile, d), lambda h, qi: (qi, h)),
            pl.BlockSpec((s, d), lambda h, qi: (0, h)),
            pl.BlockSpec((s, d), lambda h, qi: (0, v_col0 + h)),
            pl.BlockSpec((tile, LANES), lambda h, qi: (qi, 0)),
            pl.BlockSpec((FOX_HEADS, n_blk, tile), lambda h, qi: (0, 0, 0)),
        ],
        out_specs=pl.BlockSpec((tile, d), lambda h, qi: (qi, h)),
        scratch_shapes=[pltpu.VMEM((tile, LANES), F32), pltpu.VMEM((tile, LANES), F32),
                        pltpu.VMEM((tile, d), F32)],
        compiler_params=_params(("parallel", "arbitrary")),
        name="fox_attention",
    )(cend, cstart, bound, qn, kn, proj, c_col, c_row.reshape(FOX_HEADS, n_blk, tile))


def _gdn_gates_kernel(side_ref, alog_ref, dtb_ref, beta_ref, gc_ref):
    side = side_ref[...]
    tb = side.shape[0]
    beta_ref[...] = _sigmoid(side)
    g = -jnp.exp(alog_ref[...]) * _softplus(side + dtb_ref[...])
    row = lax.broadcasted_iota(jnp.int32, (tb, tb), 0)
    col = lax.broadcasted_iota(jnp.int32, (tb, tb), 1)
    tri = jnp.logical_and(row // GDN_CHUNK == col // GDN_CHUNK, col <= row).astype(F32)
    gc_ref[...] = jnp.dot(tri, g, precision=lax.Precision.HIGHEST, preferred_element_type=F32)


def _gdn_gates(side, alog_p, dtb_p, *, tb=512):
    s = side.shape[0]
    tb = min(tb, s)
    assert s % tb == 0 and tb % GDN_CHUNK == 0
    blk = pl.BlockSpec((tb, LANES), lambda i: (i, 0))
    const = pl.BlockSpec((1, LANES), lambda i: (0, 0))
    return pl.pallas_call(
        _gdn_gates_kernel,
        out_shape=[jax.ShapeDtypeStruct((s, LANES), F32), jax.ShapeDtypeStruct((s, LANES), F32)],
        grid=(s // tb,),
        in_specs=[blk, const, const],
        out_specs=[blk, blk],
        compiler_params=_params(("parallel",)),
        name="gdn_gates",
    )(side, alog_p, dtb_p)


def _conv_silu(buf, raw_ref, w_ref, tc):
    buf[HALO:HALO + tc, :] = raw_ref[...].astype(F32)
    y = jnp.zeros((tc, buf.shape[1]), F32)
    for j in range(GDN_CONV_K):
        y = y + w_ref[j:j + 1, :] * buf[pl.ds(HALO - (GDN_CONV_K - 1) + j, tc), :]
    buf[0:HALO, :] = buf[tc:tc + HALO, :]
    return _silu(y)


def _l2_norm(x):
    return x * lax.rsqrt(jnp.sum(x * x, axis=-1, keepdims=True) + NORM_EPS)


def _unit_lower_inverses(l_mats, eye, row, col):
    n = eye.shape[0]
    same_base = (row // INV_BASE) == (col // INV_BASE)
    lbs = [jnp.where(same_base, l, 0.0).astype(BF16) for l in l_mats]
    ps = [eye - lb.astype(F32) for lb in lbs]
    ms = [_dot(lb, lb) for lb in lbs]
    power = 2
    while power < INV_BASE:
        mbs = [m.astype(BF16) for m in ms]
        if 2 * power < INV_BASE:
            boths = [_dot(mb, jnp.concatenate([mb, p.astype(BF16)], axis=1)) for mb, p in zip(mbs, ps)]
            ms = [b[:, :n] for b in boths]
            xs = [b[:, n:] for b in boths]
        else:
            xs = [_dot(mb, p.astype(BF16)) for mb, p in zip(mbs, ps)]
        ps = [p + x for p, x in zip(ps, xs)]
        power *= 2
    size = INV_BASE
    while size < GDN_CHUNK:
        below = jnp.logical_and(row // (2 * size) == col // (2 * size), row // size == col // size + 1)
        c_offs = [jnp.where(below, l, 0.0).astype(BF16) for l in l_mats]
        pbs = [p.astype(BF16) for p in ps]
        ts = [_dot(c, pb).astype(BF16) for c, pb in zip(c_offs, pbs)]
        ps = [p - _dot(pb, t) for p, pb, t in zip(ps, pbs, ts)]
        size *= 2
    return ps


def _gdn_prep_kernel(q_ref, k_ref, v_ref, beta_ref, gc_ref, wq_ref, wk_ref, wv_ref,
                     g_out, n_out, qp_out, op_out, eg_out, qbuf, kbuf, vbuf, *, n_vheads):
    tc = q_ref.shape[0]
    d = GDN_HEAD_DIM
    hq = pl.program_id(0)
    rep = v_ref.shape[1] // d
    chunks_per_group = GROUP // GDN_CHUNK

    @pl.when(pl.program_id(1) == 0)
    def _():
        qbuf[0:HALO, :] = jnp.zeros((HALO, qbuf.shape[1]), F32)
        kbuf[0:HALO, :] = jnp.zeros((HALO, kbuf.shape[1]), F32)
        vbuf[0:HALO, :] = jnp.zeros((HALO, vbuf.shape[1]), F32)

    q = _l2_norm(_conv_silu(qbuf, q_ref, wq_ref, tc)) * (d ** -0.5)
    k = _l2_norm(_conv_silu(kbuf, k_ref, wk_ref, tc))
    v = _conv_silu(vbuf, v_ref, wv_ref, tc)

    beta_all = beta_ref[...]
    gc_all = gc_ref[...]
    lane = lax.broadcasted_iota(jnp.int32, beta_all.shape, 1)
    row = lax.broadcasted_iota(jnp.int32, (GROUP, GROUP), 0)
    col = lax.broadcasted_iota(jnp.int32, (GROUP, GROUP), 1)
    same_chunk = (row // GDN_CHUNK) == (col // GDN_CHUNK)
    incl = jnp.logical_and(same_chunk, col <= row)
    strict = jnp.logical_and(same_chunk, col < row)
    eye = (row == col).astype(F32)

    heads = []
    for j in range(rep):
        vh = hq * rep + j
        beta = jnp.sum(jnp.where(lane == vh, beta_all, 0.0), axis=-1, keepdims=True)
        gc = jnp.sum(jnp.where(lane == n_vheads + vh, gc_all, 0.0), axis=-1, keepdims=True)
        heads.append((beta, gc))

    probs = [(j, r) for j in range(rep) for r in range(tc // GROUP)]
    rows_of = lambda r: slice(r * GROUP, (r + 1) * GROUP)
    kgs = [k[rows_of(r)] for _, r in probs]
    qgs = [q[rows_of(r)] for _, r in probs]
    kgbs = [kg.astype(BF16) for kg in kgs]
    bgs = [heads[j][0][rows_of(r)] for j, r in probs]
    gcbs = [jnp.broadcast_to(heads[j][1][rows_of(r)], (GROUP, GROUP)) for j, r in probs]
    gcrs = [gcb.T for gcb in gcbs]
    decays = [jnp.where(incl, jnp.exp(gcb - gcr), 0.0) for gcb, gcr in zip(gcbs, gcrs)]
    kbs = [kg * bg for kg, bg in zip(kgs, bgs)]
    kqs = [_dot_nt(jnp.concatenate([kb, qg], axis=0).astype(BF16), kgb) for kb, qg, kgb in zip(kbs, qgs, kgbs)]
    l_mats = [jnp.where(strict, kq[:GROUP] * dec, 0.0) for kq, dec in zip(kqs, decays)]
    scores = [(kq[GROUP:] * dec).astype(BF16) for kq, dec in zip(kqs, decays)]
    t_invs = _unit_lower_inverses(l_mats, eye, row, col)
    egcs = [jnp.exp(gcb) for gcb in gcbs]
    rhs = [jnp.concatenate([(v[rows_of(r), j * d:(j + 1) * d] * bg).astype(BF16), (kb * egc).astype(BF16)], axis=1)
           for (j, r), bg, kb, egc in zip(probs, bgs, kbs, egcs)]
    uws = [_dot(t.astype(BF16), x) for t, x in zip(t_invs, rhs)]
    wus = [jnp.concatenate([uw[:, d:], uw[:, :d]], axis=1).astype(BF16) for uw in uws]
    g_lasts = [jnp.where(row < GDN_CHUNK, gcb[GDN_CHUNK - 1:GDN_CHUNK, :], gcb[GROUP - 1:GROUP, :]) for gcb in gcbs]
    kdts = [(kg * jnp.exp(gl - gcb)).T for kg, gl, gcb in zip(kgs, g_lasts, gcbs)]
    qos = [_dot(sc, wu) for sc, wu in zip(scores, wus)]
    gns = [[_dot(jnp.where(col // GDN_CHUNK == c, kdt, 0.0).astype(BF16), wu) for c in range(chunks_per_group)]
           for kdt, wu in zip(kdts, wus)]

    for p, (j, r) in enumerate(probs):
        cols = slice(j * d, (j + 1) * d)
        qp_out[rows_of(r), cols] = (qgs[p] * egcs[p] - qos[p][:, :d]).astype(qp_out.dtype)
        op_out[rows_of(r), cols] = qos[p][:, d:].astype(op_out.dtype)
        for c in range(chunks_per_group):
            chunk = r * chunks_per_group + c
            g_out[j, chunk * d:(chunk + 1) * d, :] = gns[p][c][:, :d].astype(g_out.dtype)
            n_out[j, chunk * d:(chunk + 1) * d, :] = gns[p][c][:, d:].astype(n_out.dtype)
            last = (c + 1) * GDN_CHUNK - 1
            eg_out[j, chunk:chunk + 1, :] = jnp.exp(gcbs[p][last:last + 1, :])


def _gdn_prep(proj, beta, gc, conv_w_t, *, n_qk, n_v, tc=512):
    s = proj.shape[0]
    d = GDN_HEAD_DIM
    rep = n_v // n_qk
    tc = min(tc, s)
    assert s % tc == 0 and tc % GROUP == 0
    kw = n_qk * d
    v_blk0 = 2 * kw // (rep * d)
    n_chunks = s // GDN_CHUNK
    cpb = tc // GDN_CHUNK
    return pl.pallas_call(
        functools.partial(_gdn_prep_kernel, n_vheads=n_v),
        out_shape=[
            jax.ShapeDtypeStruct((n_v, n_chunks * d, d), BF16),
            jax.ShapeDtypeStruct((n_v, n_chunks * d, d), BF16),
            jax.ShapeDtypeStruct((s, n_v * d), BF16),
            jax.ShapeDtypeStruct((s, n_v * d), BF16),
            jax.ShapeDtypeStruct((n_v, n_chunks, LANES), F32),
        ],
        grid=(n_qk, s // tc),
        in_specs=[
            pl.BlockSpec((tc, d), lambda h, i: (i, h)),
            pl.BlockSpec((tc, d), lambda h, i: (i, n_qk + h)),
            pl.BlockSpec((tc, rep * d), lambda h, i: (i, v_blk0 + h)),
            pl.BlockSpec((tc, LANES), lambda h, i: (i, 0)),
            pl.BlockSpec((tc, LANES), lambda h, i: (i, 0)),
            pl.BlockSpec((GDN_CONV_K, d), lambda h, i: (0, h)),
            pl.BlockSpec((GDN_CONV_K, d), lambda h, i: (0, n_qk + h)),
            pl.BlockSpec((GDN_CONV_K, rep * d), lambda h, i: (0, v_blk0 + h)),
        ],
        out_specs=[
            pl.BlockSpec((rep, cpb * d, d), lambda h, i: (h, i, 0)),
            pl.BlockSpec((rep, cpb * d, d), lambda h, i: (h, i, 0)),
            pl.BlockSpec((tc, rep * d), lambda h, i: (i, h)),
            pl.BlockSpec((tc, rep * d), lambda h, i: (i, h)),
            pl.BlockSpec((rep, cpb, LANES), lambda h, i: (h, i, 0)),
        ],
        scratch_shapes=[pltpu.VMEM((tc + HALO, d), F32), pltpu.VMEM((tc + HALO, d), F32),
                        pltpu.VMEM((tc + HALO, rep * d), F32)],
        compiler_params=_params(("parallel", "arbitrary")),
        name="gdn_prep",
    )(proj, proj, proj, beta, gc, conv_w_t, conv_w_t, conv_w_t)


def _gdn_rec_kernel(g_ref, n_ref, qp_ref, op_ref, eg_ref, z_ref, on_ref, o_ref, state, obuf):
    hb = g_ref.shape[0]
    tr = qp_ref.shape[0]
    d = GDN_HEAD_DIM

    @pl.when(pl.program_id(1) == 0)
    def _():
        state[...] = jnp.zeros_like(state)

    def chunk_step(c, carry):
        r0 = pl.multiple_of(c * d, d)
        q0 = pl.multiple_of(c * GDN_CHUNK, GDN_CHUNK)
        sbs = [state[j].astype(BF16) for j in range(hb)]
        gs = [_dot(g_ref[j, pl.ds(r0, d), :], sbs[j]) for j in range(hb)]
        os = [_dot(qp_ref[pl.ds(q0, GDN_CHUNK), j * d:(j + 1) * d], sbs[j]) for j in range(hb)]
        for j in range(hb):
            state[j] = eg_ref[j, pl.ds(c, 1), :] * state[j] - gs[j] + n_ref[j, pl.ds(r0, d), :].astype(F32)
            obuf[pl.ds(q0, GDN_CHUNK), j * d:(j + 1) * d] = (
                os[j] + op_ref[pl.ds(q0, GDN_CHUNK), j * d:(j + 1) * d].astype(F32))
        return carry

    lax.fori_loop(0, tr // GDN_CHUNK, chunk_step, 0)

    for j in range(hb):
        cols = slice(j * d, (j + 1) * d)
        o = obuf[:, cols]
        ms = jnp.mean(o * o, axis=-1, keepdims=True)
        o_ref[:, cols] = (o * lax.rsqrt(ms + NORM_EPS) * on_ref[...] * _silu(z_ref[:, cols].astype(F32))
                          ).astype(o_ref.dtype)


def _gdn_recurrence(g_op, n_op, qp, op, eg, proj, z_col0, out_norm, *, hb=8, tr=512):
    n_v, rows, d = g_op.shape
    s = qp.shape[0]
    hb = min(hb, n_v)
    tr = min(tr, s)
    assert n_v % hb == 0 and s % tr == 0 and z_col0 % (hb * d) == 0
    cpb = tr // GDN_CHUNK
    z_blk0 = z_col0 // (hb * d)
    return pl.pallas_call(
        _gdn_rec_kernel,
        out_shape=jax.ShapeDtypeStruct((s, n_v * d), BF16),
        grid=(n_v // hb, s // tr),
        in_specs=[
            pl.BlockSpec((hb, cpb * d, d), lambda h, i: (h, i, 0)),
            pl.BlockSpec((hb, cpb * d, d), lambda h, i: (h, i, 0)),
            pl.BlockSpec((tr, hb * d), lambda h, i: (i, h)),
            pl.BlockSpec((tr, hb * d), lambda h, i: (i, h)),
            pl.BlockSpec((hb, cpb, LANES), lambda h, i: (h, i, 0)),
            pl.BlockSpec((tr, hb * d), lambda h, i: (i, z_blk0 + h)),
            pl.BlockSpec((1, d), lambda h, i: (0, 0)),
        ],
        out_specs=pl.BlockSpec((tr, hb * d), lambda h, i: (i, h)),
        scratch_shapes=[pltpu.VMEM((hb, d, d), F32), pltpu.VMEM((tr, hb * d), F32)],
        compiler_params=_params(("parallel", "arbitrary")),
        name="gdn_recurrence",
    )(g_op, n_op, qp, op, eg, proj, out_norm.reshape(1, d))


def _pad_lanes(vec, offset=0):
    return jnp.zeros((1, LANES), F32).at[0, offset:offset + vec.shape[0]].set(vec.astype(F32))


def kernel(x, norm_mix, norm_mlp, w_mlp_in, w_mlp_out, ab_w_in, ab_conv_w, fox_f_bias, fox_q_norm,
           fox_k_norm, ab_w_out, gdn_w_in, gdn_conv_w, gdn_a_log, gdn_dt_bias, gdn_out_norm, gdn_w_out):
    bsz, s_len, d_model = x.shape
    depth = norm_mix.shape[0]
    width = ab_conv_w.shape[1]
    n_v = gdn_a_log.shape[1]
    qkv = gdn_conv_w.shape[1]
    n_qk = (qkv - n_v * GDN_HEAD_DIM) // (2 * GDN_HEAD_DIM)
    ab_main = 6 * width
    gdn_main = qkv + n_v * GDN_HEAD_DIM
    d_ff = w_mlp_in.shape[2]
    assert ab_w_in.shape[2] == ab_main + FOX_HEADS and gdn_w_in.shape[2] == gdn_main + 2 * n_v
    assert 2 * n_v <= LANES and FOX_HEADS <= LANES

    def side_weight(w_stack, layer, start):
        cols = lax.slice(w_stack, (layer, 0, start), (layer + 1, w_stack.shape[1], w_stack.shape[2]))[0]
        return jnp.pad(cols, ((0, 0), (0, LANES - cols.shape[1])))

    ab_w_main = ab_w_in[:, :, :ab_main].astype(BF16)
    gdn_w_main = gdn_w_in[:, :, :gdn_main].astype(BF16)

    outs = []
    for b in range(bsz):
        xb = x[b]
        for layer in range(depth):
            i = layer // 2
            if layer % 2 == 0:
                proj, fl = _norm_matmul(xb, norm_mix[layer], ab_w_main, i, ab_main,
                                        side_w=side_weight(ab_w_in, i, ab_main), name="ab_in_proj")
                y_a, qn, kn, c_col, c_row, qmax, kmax = _fox_prep(
                    proj, fl, ab_conv_w[i].T, _pad_lanes(fox_f_bias[i]), fox_q_norm[i], fox_k_norm[i], width=width)
                y_b = _fox_attention(qn, kn, proj, 5 * FOX_HEADS, c_col, c_row, qmax, kmax)
                xb = _matmul_res([y_a, y_b], ab_w_out, i, xb, name="ab_out_proj")
            else:
                proj, side = _norm_matmul(xb, norm_mix[layer], gdn_w_main, i, gdn_main,
                                          side_w=side_weight(gdn_w_in, i, gdn_main), name="gdn_in_proj")
                beta, gc = _gdn_gates(side, _pad_lanes(gdn_a_log[i], n_v), _pad_lanes(gdn_dt_bias[i], n_v))
                g_op, n_op, qp, op, eg = _gdn_prep(proj, beta, gc, gdn_conv_w[i].T, n_qk=n_qk, n_v=n_v)
                o = _gdn_recurrence(g_op, n_op, qp, op, eg, proj, qkv, gdn_out_norm[i])
                xb = _matmul_res([o], gdn_w_out, i, xb, name="gdn_out_proj")
            hidden = _norm_matmul(xb, norm_mlp[layer], w_mlp_in, layer, d_ff, act=True, name="mlp_in")
            xb = _matmul_res([hidden], w_mlp_out, layer, xb, name="mlp_out")
        outs.append(xb)
    return outs[0][None] if bsz == 1 else jnp.stack(outs, axis=0)
```

```python
import functools

import jax
import jax.numpy as jnp
from jax import lax
from jax.experimental import pallas as pl
from jax.experimental.pallas import tpu as pltpu

F32 = jnp.float32
BF16 = jnp.bfloat16

NORM_EPS = 1e-6
FOX_HEADS = 8
CONV_A_K = 3
GDN_HEAD_DIM = 128
GDN_CONV_K = 4
GDN_CHUNK = 64
LANES = 128
HALO = 8
GROUP = 2 * GDN_CHUNK
INV_BASE = 16
VMEM_LIMIT = 60000 * 1024
LOG2E = 1.4426950408889634
PRUNE_LOG2 = 160.0
FAST_LOG2 = 100.0
NORM_MARGIN = 1.001


def _params(semantics, vmem=VMEM_LIMIT):
    return pltpu.CompilerParams(dimension_semantics=semantics, vmem_limit_bytes=vmem)


def _dot(a, b):
    return jnp.dot(a, b, preferred_element_type=F32)


def _dot_nt(a, b):
    return lax.dot_general(a, b, (((1,), (1,)), ((), ())), preferred_element_type=F32)


def _sigmoid(x):
    return 0.5 * jnp.tanh(0.5 * x) + 0.5


def _silu(x):
    return x * _sigmoid(x)


def _softplus(x):
    return jnp.maximum(x, 0.0) + jnp.log1p(jnp.exp(-jnp.abs(x)))


def _norm_matmul_kernel(*refs, act, has_side):
    if has_side:
        x_ref, g_ref, w_ref, ws_ref, o_ref, side_ref, h_ref = refs
    else:
        x_ref, g_ref, w_ref, o_ref, h_ref = refs

    @pl.when(pl.program_id(1) == 0)
    def _():
        xf = x_ref[...]
        ms = jnp.mean(xf * xf, axis=-1, keepdims=True)
        h = (xf * lax.rsqrt(ms + NORM_EPS) * g_ref[...]).astype(BF16)
        h_ref[...] = h
        if has_side:
            side_ref[...] = _dot(h, ws_ref[...].astype(BF16))

    acc = _dot(h_ref[...], w_ref[...].astype(BF16))
    if act:
        acc = jnp.square(jnp.maximum(acc, 0.0))
    o_ref[...] = acc.astype(o_ref.dtype)


STREAMED_W_BYTES = 8 * 1024 * 1024


def _norm_matmul(x, g, w_stack, layer, n, *, act=False, side_w=None, out_dtype=BF16, tm=1024, name):
    m, k = x.shape
    tn = n
    while (k * tn * w_stack.dtype.itemsize > STREAMED_W_BYTES or n % tn) and tn % 256 == 0:
        tn //= 2
    tm = min(tm, m)
    assert m % tm == 0 and n % tn == 0 and w_stack.shape[1] == k
    in_specs = [
        pl.BlockSpec((tm, k), lambda i, j: (i, 0)),
        pl.BlockSpec((1, k), lambda i, j: (0, 0)),
        pl.BlockSpec((None, k, tn), lambda i, j: (layer, 0, j)),
    ]
    args = [x, g.reshape(1, k), w_stack]
    out_shape = [jax.ShapeDtypeStruct((m, n), out_dtype)]
    out_specs = [pl.BlockSpec((tm, tn), lambda i, j: (i, j))]
    if side_w is not None:
        in_specs.append(pl.BlockSpec((k, LANES), lambda i, j: (0, 0)))
        args.append(side_w)
        out_shape.append(jax.ShapeDtypeStruct((m, LANES), F32))
        out_specs.append(pl.BlockSpec((tm, LANES), lambda i, j: (i, 0)))
    res = pl.pallas_call(
        functools.partial(_norm_matmul_kernel, act=act, has_side=side_w is not None),
        out_shape=out_shape,
        grid=(m // tm, n // tn),
        in_specs=in_specs,
        out_specs=out_specs,
        scratch_shapes=[pltpu.VMEM((tm, k), BF16)],
        compiler_params=_params(("parallel", "arbitrary")),
        name=name,
    )(*args)
    return res if side_w is not None else res[0]


def _cast_through_specs(stack, layer, grid):
    rows, cols = stack.shape[1:]
    steps = grid[0] * grid[1]
    slab = rows // steps
    assert slab * steps == rows and slab % 16 == 0
    step = lambda a, b: a * grid[1] + b
    in_spec = pl.BlockSpec((None, slab, cols), lambda a, b: (layer, step(a, b), 0))
    out_spec = pl.BlockSpec((None, slab, cols), lambda a, b: (0, step(a, b), 0))
    return in_spec, out_spec, jax.ShapeDtypeStruct((1, rows, cols), BF16)


def _matmul_res_kernel(*refs, n_a, cast_w):
    a_refs = refs[:n_a]
    if cast_w:
        w_ref, r_ref, o_ref, wb_ref = refs[n_a:]

        @pl.when(pl.program_id(1) == 0)
        def _():
            wb_ref[...] = w_ref[...].astype(BF16)
    else:
        wb_ref, r_ref, o_ref = refs[n_a:]

    acc = r_ref[...]
    off = 0
    for a_ref in a_refs:
        ka = a_ref.shape[1]
        acc = acc + _dot(a_ref[...], wb_ref[off:off + ka, :])
        off += ka
    o_ref[...] = acc


RESIDENT_W_BYTES = 32 * 1024 * 1024
PIPELINE_MARGIN_BYTES = 4 * 1024 * 1024
MAX_ROW_BLOCK = 512


def _matmul_res(a_list, w_stack, layer, res, *, name):
    m = res.shape[0]
    k, n = w_stack.shape[1:]
    cast_w = w_stack.dtype != BF16
    w_bytes = w_stack.dtype.itemsize + (2 if cast_w else 0)
    tn = n
    while k * tn * w_bytes > RESIDENT_W_BYTES and tn % 2 == 0:
        tn //= 2
    row_bytes = 2 * k * 2 + 4 * tn * 4
    tm = min(MAX_ROW_BLOCK, m)
    while tm * row_bytes > VMEM_LIMIT - k * tn * w_bytes - PIPELINE_MARGIN_BYTES and tm % 2 == 0:
        tm //= 2
    assert m % tm == 0 and n % tn == 0 and sum(a.shape[1] for a in a_list) == k
    return pl.pallas_call(
        functools.partial(_matmul_res_kernel, n_a=len(a_list), cast_w=cast_w),
        out_shape=jax.ShapeDtypeStruct((m, n), F32),
        grid=(n // tn, m // tm),
        in_specs=[pl.BlockSpec((tm, a.shape[1]), lambda j, i: (i, 0)) for a in a_list] + [
            pl.BlockSpec((None, k, tn), lambda j, i: (layer, 0, j), pipeline_mode=pl.Buffered(1)),
            pl.BlockSpec((tm, tn), lambda j, i: (i, j)),
        ],
        out_specs=pl.BlockSpec((tm, tn), lambda j, i: (i, j)),
        scratch_shapes=[pltpu.VMEM((k, tn), BF16)] if cast_w else [],
        compiler_params=_params(("parallel", "arbitrary")),
        name=name,
    )(*a_list, w_stack, res)


def _fox_prep_kernel(gb_ref, gc_ref, xv_ref, q_ref, k_ref, fl_ref, cw_ref, fb_ref, qg_ref, kg_ref,
                     ya_ref, qn_ref, kn_ref, c_ref, ct_ref, qmax_ref, kmax_ref, ubuf, carry, *, head_dim, scale):
    tb = gb_ref.shape[0]
    first = pl.program_id(0) == 0

    @pl.when(first)
    def _():
        ubuf[0:HALO, :] = jnp.zeros((HALO, ubuf.shape[1]), F32)
        carry[...] = jnp.zeros_like(carry)
        qmax_ref[...] = jnp.zeros_like(qmax_ref)
        kmax_ref[...] = jnp.zeros_like(kmax_ref)

    ubuf[HALO:HALO + tb, :] = gc_ref[...].astype(F32) * xv_ref[...].astype(F32)
    y = jnp.zeros((tb, ubuf.shape[1]), F32)
    for j in range(CONV_A_K):
        y = y + cw_ref[j:j + 1, :] * ubuf[pl.ds(HALO - (CONV_A_K - 1) + j, tb), :]
    ya_ref[...] = (gb_ref[...].astype(F32) * y).astype(ya_ref.dtype)
    ubuf[0:HALO, :] = ubuf[tb:tb + HALO, :]

    for h in range(q_ref.shape[1] // head_dim):
        sl = slice(h * head_dim, (h + 1) * head_dim)
        qh = q_ref[:, sl].astype(F32)
        kh = k_ref[:, sl].astype(F32)
        qms = jnp.mean(qh * qh, axis=-1, keepdims=True)
        kms = jnp.mean(kh * kh, axis=-1, keepdims=True)
        qn = (qh * lax.rsqrt(qms + NORM_EPS) * (qg_ref[...] * scale)).astype(qn_ref.dtype)
        kn = (kh * lax.rsqrt(kms + NORM_EPS) * kg_ref[...]).astype(kn_ref.dtype)
        qn_ref[:, sl] = qn
        kn_ref[:, sl] = kn
        qf, kf = qn.astype(F32), kn.astype(F32)
        q_norm = jnp.sqrt(jnp.max(jnp.sum(qf * qf, axis=-1, keepdims=True), axis=0, keepdims=True))
        k_norm = jnp.sqrt(jnp.max(jnp.sum(kf * kf, axis=-1, keepdims=True), axis=0, keepdims=True))
        qmax_ref[h:h + 1, :] = jnp.maximum(qmax_ref[h:h + 1, :], q_norm)
        kmax_ref[h:h + 1, :] = jnp.maximum(kmax_ref[h:h + 1, :], k_norm)

    z = fl_ref[...] + fb_ref[...]
    log_f = jnp.minimum(z, 0.0) - jnp.log1p(jnp.exp(-jnp.abs(z)))
    row = lax.broadcasted_iota(jnp.int32, (tb, tb), 0)
    col = lax.broadcasted_iota(jnp.int32, (tb, tb), 1)
    tri = (col <= row).astype(F32)
    c = jnp.dot(tri, log_f, precision=lax.Precision.HIGHEST, preferred_element_type=F32) + carry[...]
    c_ref[...] = c
    ct_ref[...] = c.T[0:ct_ref.shape[0], :]
    carry[...] = c[tb - 1:tb, :]


def _fox_prep(proj, fl, conv_w_t, f_bias_p, q_norm, k_norm, *, width, tb=512):
    s = proj.shape[0]
    tb = min(tb, s)
    head_dim = width // FOX_HEADS
    col = lambda c: pl.BlockSpec((tb, width), lambda i, c=c: (i, c))
    const = lambda shape: pl.BlockSpec(shape, lambda i: (0, 0))
    return pl.pallas_call(
        functools.partial(_fox_prep_kernel, head_dim=head_dim, scale=LOG2E * head_dim ** -0.5),
        out_shape=[
            jax.ShapeDtypeStruct((s, width), BF16),
            jax.ShapeDtypeStruct((s, width), BF16),
            jax.ShapeDtypeStruct((s, width), BF16),
            jax.ShapeDtypeStruct((s, LANES), F32),
            jax.ShapeDtypeStruct((FOX_HEADS, s), F32),
            jax.ShapeDtypeStruct((FOX_HEADS, LANES), F32),
            jax.ShapeDtypeStruct((FOX_HEADS, LANES), F32),
        ],
        grid=(s // tb,),
        in_specs=[col(0), col(1), col(2), col(3), col(4),
                  pl.BlockSpec((tb, LANES), lambda i: (i, 0)),
                  const((CONV_A_K, width)), const((1, LANES)),
                  const((1, head_dim)), const((1, head_dim))],
        out_specs=[
            pl.BlockSpec((tb, width), lambda i: (i, 0)),
            pl.BlockSpec((tb, width), lambda i: (i, 0)),
            pl.BlockSpec((tb, width), lambda i: (i, 0)),
            pl.BlockSpec((tb, LANES), lambda i: (i, 0)),
            pl.BlockSpec((FOX_HEADS, tb), lambda i: (0, i)),
            const((FOX_HEADS, LANES)), const((FOX_HEADS, LANES)),
        ],
        scratch_shapes=[pltpu.VMEM((tb + HALO, width), F32), pltpu.VMEM((1, LANES), F32)],
        compiler_params=_params(("arbitrary",)),
        name="fox_prep",
    )(proj, proj, proj, proj, proj, fl, conv_w_t, f_bias_p, q_norm.reshape(1, -1), k_norm.reshape(1, -1))


def _fox_attn_kernel(cend_ref, cstart_ref, bound_ref, q_ref, k_ref, v_ref, cq_ref, ck_ref, wf_ref, o_ref, wb_ref,
                     m_ref, l_ref, acc_ref, *, tile):
    wb_ref[...] = wf_ref[...].astype(BF16)
    h = pl.program_id(0)
    qi = pl.program_id(1)
    reps = tile // LANES
    lane = lax.broadcasted_iota(jnp.int32, (tile, LANES), 1)
    cq = jnp.sum(jnp.where(lane == h, cq_ref[...], 0.0), axis=-1, keepdims=True) * LOG2E
    cq = jnp.broadcast_to(cq, (tile, LANES))
    q = q_ref[...]

    def scores(j):
        start = pl.multiple_of(j * tile, tile)
        ck = ck_ref[h, pl.ds(j, 1), :] * LOG2E
        return _dot_nt(q, k_ref[pl.ds(start, tile), :]) - ck, start

    def accumulate(s, start, stab):
        p = jnp.exp2(s - jnp.concatenate([stab] * reps, axis=1))
        part = p[:, 0:LANES]
        for r in range(1, reps):
            part = part + p[:, r * LANES:(r + 1) * LANES]
        l_ref[...] += part
        acc_ref[...] += _dot(p.astype(v_ref.dtype), v_ref[pl.ds(start, tile), :])

    def online_step(j, masked):
        s, start = scores(j)
        if masked:
            q_pos = lax.broadcasted_iota(jnp.int32, (tile, tile), 0)
            k_pos = lax.broadcasted_iota(jnp.int32, (tile, tile), 1)
            s = jnp.where(k_pos <= q_pos, s, -jnp.inf)
        m_old = m_ref[...]
        m_new = jnp.maximum(m_old, jnp.broadcast_to(jnp.max(s, axis=-1, keepdims=True), (tile, LANES)) + cq)
        alpha = jnp.exp2(m_old - m_new)
        l_ref[...] = alpha * l_ref[...]
        acc_ref[...] = alpha * acc_ref[...]
        m_ref[...] = m_new
        accumulate(s, start, m_new - cq)

    m_ref[...] = jnp.full_like(m_ref, -jnp.inf)
    l_ref[...] = jnp.zeros_like(l_ref)
    acc_ref[...] = jnp.zeros_like(acc_ref)
    online_step(qi, True)

    two_b = 2.0 * bound_ref[h]
    reach = two_b + PRUNE_LOG2
    c_tile = cstart_ref[h, qi]

    def in_reach(j):
        return jnp.logical_and(j >= 0, cend_ref[h, jnp.maximum(j, 0)] - c_tile <= reach)

    def fast_block(j):
        s, start = scores(j)
        accumulate(s, start, m_ref[...] - cq)
        return j - 1

    def online_block(j):
        online_step(j, False)
        return j - 1

    @pl.when(two_b <= FAST_LOG2)
    def _():
        lax.while_loop(in_reach, fast_block, qi - 1)

    @pl.when(two_b > FAST_LOG2)
    def _():
        lax.while_loop(in_reach, online_block, qi - 1)

    o_ref[...] = (acc_ref[...] / jnp.sum(l_ref[...], axis=-1, keepdims=True)).astype(o_ref.dtype)


def _fox_attention(qn, kn, proj, v_col0, c_col, c_row, qmax, kmax, cast, *, tile=512):
    s, width = qn.shape
    d = width // FOX_HEADS
    tile = min(tile, s)
    assert d == LANES and s % tile == 0 and tile % LANES == 0
    n_blk = s // tile
    cast_in, cast_out, cast_shape = _cast_through_specs(*cast, (FOX_HEADS, n_blk))
    cend = c_row[:, tile - 1::tile] * LOG2E
    cstart = c_row[:, ::tile] * LOG2E
    bound = qmax[:, 0] * kmax[:, 0] * NORM_MARGIN
    smem = pl.BlockSpec(memory_space=pltpu.SMEM)
    return pl.pallas_call(
        functools.partial(_fox_attn_kernel, tile=tile),
        out_shape=[jax.ShapeDtypeStruct((s, width), BF16), cast_shape],
        grid=(FOX_HEADS, n_blk),
        in_specs=[
            smem, smem, smem,
            pl.BlockSpec((tile, d), lambda h, qi: (qi, h)),
            pl.BlockSpec((s, d), lambda h, qi: (0, h)),
            pl.BlockSpec((s, d), lambda h, qi: (0, v_col0 + h)),
            pl.BlockSpec((tile, LANES), lambda h, qi: (qi, 0)),
            pl.BlockSpec((FOX_HEADS, n_blk, tile), lambda h, qi: (0, 0, 0)),
            cast_in,
        ],
        out_specs=[pl.BlockSpec((tile, d), lambda h, qi: (qi, h)), cast_out],
        scratch_shapes=[pltpu.VMEM((tile, LANES), F32), pltpu.VMEM((tile, LANES), F32),
                        pltpu.VMEM((tile, d), F32)],
        compiler_params=_params(("parallel", "arbitrary")),
        name="fox_attention",
    )(cend, cstart, bound, qn, kn, proj, c_col, c_row.reshape(FOX_HEADS, n_blk, tile), cast[0])


def _gdn_gates_kernel(side_ref, alog_ref, dtb_ref, beta_ref, gc_ref):
    side = side_ref[...]
    tb = side.shape[0]
    beta_ref[...] = _sigmoid(side)
    g = -jnp.exp(alog_ref[...]) * _softplus(side + dtb_ref[...])
    row = lax.broadcasted_iota(jnp.int32, (tb, tb), 0)
    col = lax.broadcasted_iota(jnp.int32, (tb, tb), 1)
    tri = jnp.logical_and(row // GDN_CHUNK == col // GDN_CHUNK, col <= row).astype(F32)
    gc_ref[...] = jnp.dot(tri, g, precision=lax.Precision.HIGHEST, preferred_element_type=F32)


def _gdn_gates(side, alog_p, dtb_p, *, tb=512):
    s = side.shape[0]
    tb = min(tb, s)
    assert s % tb == 0 and tb % GDN_CHUNK == 0
    blk = pl.BlockSpec((tb, LANES), lambda i: (i, 0))
    const = pl.BlockSpec((1, LANES), lambda i: (0, 0))
    return pl.pallas_call(
        _gdn_gates_kernel,
        out_shape=[jax.ShapeDtypeStruct((s, LANES), F32), jax.ShapeDtypeStruct((s, LANES), F32)],
        grid=(s // tb,),
        in_specs=[blk, const, const],
        out_specs=[blk, blk],
        compiler_params=_params(("parallel",)),
        name="gdn_gates",
    )(side, alog_p, dtb_p)


def _conv_silu(buf, raw_ref, w_ref, tc):
    buf[HALO:HALO + tc, :] = raw_ref[...].astype(F32)
    y = jnp.zeros((tc, buf.shape[1]), F32)
    for j in range(GDN_CONV_K):
        y = y + w_ref[j:j + 1, :] * buf[pl.ds(HALO - (GDN_CONV_K - 1) + j, tc), :]
    buf[0:HALO, :] = buf[tc:tc + HALO, :]
    return _silu(y)


def _l2_norm(x):
    return x * lax.rsqrt(jnp.sum(x * x, axis=-1, keepdims=True) + NORM_EPS)


def _unit_lower_inverses(l_mats, eye, row, col):
    n = eye.shape[0]
    same_base = (row // INV_BASE) == (col // INV_BASE)
    lbs = [jnp.where(same_base, l, 0.0).astype(BF16) for l in l_mats]
    ps = [eye - lb.astype(F32) for lb in lbs]
    ms = [_dot(lb, lb) for lb in lbs]
    power = 2
    while power < INV_BASE:
        mbs = [m.astype(BF16) for m in ms]
        if 2 * power < INV_BASE:
            boths = [_dot(mb, jnp.concatenate([mb, p.astype(BF16)], axis=1)) for mb, p in zip(mbs, ps)]
            ms = [b[:, :n] for b in boths]
            xs = [b[:, n:] for b in boths]
        else:
            xs = [_dot(mb, p.astype(BF16)) for mb, p in zip(mbs, ps)]
        ps = [p + x for p, x in zip(ps, xs)]
        power *= 2
    size = INV_BASE
    while size < GDN_CHUNK:
        below = jnp.logical_and(row // (2 * size) == col // (2 * size), row // size == col // size + 1)
        c_offs = [jnp.where(below, l, 0.0).astype(BF16) for l in l_mats]
        pbs = [p.astype(BF16) for p in ps]
        ts = [_dot(c, pb).astype(BF16) for c, pb in zip(c_offs, pbs)]
        ps = [p - _dot(pb, t) for p, pb, t in zip(ps, pbs, ts)]
        size *= 2
    return ps


def _gdn_prep_kernel(q_ref, k_ref, v_ref, beta_ref, gc_ref, wq_ref, wk_ref, wv_ref, wf_ref,
                     g_out, n_out, qp_out, op_out, eg_out, wb_ref, qbuf, kbuf, vbuf, *, n_vheads):
    wb_ref[...] = wf_ref[...].astype(BF16)
    tc = q_ref.shape[0]
    d = GDN_HEAD_DIM
    hq = pl.program_id(0)
    rep = v_ref.shape[1] // d
    chunks_per_group = GROUP // GDN_CHUNK

    @pl.when(pl.program_id(1) == 0)
    def _():
        qbuf[0:HALO, :] = jnp.zeros((HALO, qbuf.shape[1]), F32)
        kbuf[0:HALO, :] = jnp.zeros((HALO, kbuf.shape[1]), F32)
        vbuf[0:HALO, :] = jnp.zeros((HALO, vbuf.shape[1]), F32)

    q = _l2_norm(_conv_silu(qbuf, q_ref, wq_ref, tc)) * (d ** -0.5)
    k = _l2_norm(_conv_silu(kbuf, k_ref, wk_ref, tc))
    v = _conv_silu(vbuf, v_ref, wv_ref, tc)

    beta_all = beta_ref[...]
    gc_all = gc_ref[...]
    lane = lax.broadcasted_iota(jnp.int32, beta_all.shape, 1)
    row = lax.broadcasted_iota(jnp.int32, (GROUP, GROUP), 0)
    col = lax.broadcasted_iota(jnp.int32, (GROUP, GROUP), 1)
    same_chunk = (row // GDN_CHUNK) == (col // GDN_CHUNK)
    incl = jnp.logical_and(same_chunk, col <= row)
    strict = jnp.logical_and(same_chunk, col < row)
    eye = (row == col).astype(F32)

    heads = []
    for j in range(rep):
        vh = hq * rep + j
        beta = jnp.sum(jnp.where(lane == vh, beta_all, 0.0), axis=-1, keepdims=True)
        gc = jnp.sum(jnp.where(lane == n_vheads + vh, gc_all, 0.0), axis=-1, keepdims=True)
        heads.append((beta, gc))

    probs = [(j, r) for j in range(rep) for r in range(tc // GROUP)]
    rows_of = lambda r: slice(r * GROUP, (r + 1) * GROUP)
    kgs = [k[rows_of(r)] for _, r in probs]
    qgs = [q[rows_of(r)] for _, r in probs]
    kgbs = [kg.astype(BF16) for kg in kgs]
    bgs = [heads[j][0][rows_of(r)] for j, r in probs]
    gcbs = [jnp.broadcast_to(heads[j][1][rows_of(r)], (GROUP, GROUP)) for j, r in probs]
    gcrs = [gcb.T for gcb in gcbs]
    decays = [jnp.where(incl, jnp.exp(gcb - gcr), 0.0) for gcb, gcr in zip(gcbs, gcrs)]
    kbs = [kg * bg for kg, bg in zip(kgs, bgs)]
    kqs = [_dot_nt(jnp.concatenate([kb, qg], axis=0).astype(BF16), kgb) for kb, qg, kgb in zip(kbs, qgs, kgbs)]
    l_mats = [jnp.where(strict, kq[:GROUP] * dec, 0.0) for kq, dec in zip(kqs, decays)]
    scores = [(kq[GROUP:] * dec).astype(BF16) for kq, dec in zip(kqs, decays)]
    t_invs = _unit_lower_inverses(l_mats, eye, row, col)
    egcs = [jnp.exp(gcb) for gcb in gcbs]
    rhs = [jnp.concatenate([(v[rows_of(r), j * d:(j + 1) * d] * bg).astype(BF16), (kb * egc).astype(BF16)], axis=1)
           for (j, r), bg, kb, egc in zip(probs, bgs, kbs, egcs)]
    uws = [_dot(t.astype(BF16), x) for t, x in zip(t_invs, rhs)]
    wus = [jnp.concatenate([uw[:, d:], uw[:, :d]], axis=1).astype(BF16) for uw in uws]
    g_lasts = [jnp.where(row < GDN_CHUNK, gcb[GDN_CHUNK - 1:GDN_CHUNK, :], gcb[GROUP - 1:GROUP, :]) for gcb in gcbs]
    kdts = [(kg * jnp.exp(gl - gcb)).T for kg, gl, gcb in zip(kgs, g_lasts, gcbs)]
    qos = [_dot(sc, wu) for sc, wu in zip(scores, wus)]
    gns = [[_dot(jnp.where(col // GDN_CHUNK == c, kdt, 0.0).astype(BF16), wu) for c in range(chunks_per_group)]
           for kdt, wu in zip(kdts, wus)]

    for p, (j, r) in enumerate(probs):
        cols = slice(j * d, (j + 1) * d)
        qp_out[rows_of(r), cols] = (qgs[p] * egcs[p] - qos[p][:, :d]).astype(qp_out.dtype)
        op_out[rows_of(r), cols] = qos[p][:, d:].astype(op_out.dtype)
        for c in range(chunks_per_group):
            chunk = r * chunks_per_group + c
            g_out[j, chunk * d:(chunk + 1) * d, :] = gns[p][c][:, :d].astype(g_out.dtype)
            n_out[j, chunk * d:(chunk + 1) * d, :] = gns[p][c][:, d:].astype(n_out.dtype)
            last = (c + 1) * GDN_CHUNK - 1
            eg_out[j, chunk:chunk + 1, :] = jnp.exp(gcbs[p][last:last + 1, :])


def _gdn_prep(proj, beta, gc, conv_w_t, cast, *, n_qk, n_v, tc=512):
    s = proj.shape[0]
    d = GDN_HEAD_DIM
    rep = n_v // n_qk
    tc = min(tc, s)
    assert s % tc == 0 and tc % GROUP == 0
    kw = n_qk * d
    v_blk0 = 2 * kw // (rep * d)
    n_chunks = s // GDN_CHUNK
    cpb = tc // GDN_CHUNK
    cast_in, cast_out, cast_shape = _cast_through_specs(*cast, (n_qk, s // tc))
    return pl.pallas_call(
        functools.partial(_gdn_prep_kernel, n_vheads=n_v),
        out_shape=[
            jax.ShapeDtypeStruct((n_v, n_chunks * d, d), BF16),
            jax.ShapeDtypeStruct((n_v, n_chunks * d, d), BF16),
            jax.ShapeDtypeStruct((s, n_v * d), BF16),
            jax.ShapeDtypeStruct((s, n_v * d), BF16),
            jax.ShapeDtypeStruct((n_v, n_chunks, LANES), F32),
            cast_shape,
        ],
        grid=(n_qk, s // tc),
        in_specs=[
            pl.BlockSpec((tc, d), lambda h, i: (i, h)),
            pl.BlockSpec((tc, d), lambda h, i: (i, n_qk + h)),
            pl.BlockSpec((tc, rep * d), lambda h, i: (i, v_blk0 + h)),
            pl.BlockSpec((tc, LANES), lambda h, i: (i, 0)),
            pl.BlockSpec((tc, LANES), lambda h, i: (i, 0)),
            pl.BlockSpec((GDN_CONV_K, d), lambda h, i: (0, h)),
            pl.BlockSpec((GDN_CONV_K, d), lambda h, i: (0, n_qk + h)),
            pl.BlockSpec((GDN_CONV_K, rep * d), lambda h, i: (0, v_blk0 + h)),
            cast_in,
        ],
        out_specs=[
            pl.BlockSpec((rep, cpb * d, d), lambda h, i: (h, i, 0)),
            pl.BlockSpec((rep, cpb * d, d), lambda h, i: (h, i, 0)),
            pl.BlockSpec((tc, rep * d), lambda h, i: (i, h)),
            pl.BlockSpec((tc, rep * d), lambda h, i: (i, h)),
            pl.BlockSpec((rep, cpb, LANES), lambda h, i: (h, i, 0)),
            cast_out,
        ],
        scratch_shapes=[pltpu.VMEM((tc + HALO, d), F32), pltpu.VMEM((tc + HALO, d), F32),
                        pltpu.VMEM((tc + HALO, rep * d), F32)],
        compiler_params=_params(("parallel", "arbitrary")),
        name="gdn_prep",
    )(proj, proj, proj, beta, gc, conv_w_t, conv_w_t, conv_w_t, cast[0])


def _gdn_rec_kernel(g_ref, n_ref, qp_ref, op_ref, eg_ref, z_ref, on_ref, o_ref, state, obuf):
    hb = g_ref.shape[0]
    tr = qp_ref.shape[0]
    d = GDN_HEAD_DIM

    @pl.when(pl.program_id(1) == 0)
    def _():
        state[...] = jnp.zeros_like(state)

    def chunk_step(c, carry):
        r0 = pl.multiple_of(c * d, d)
        q0 = pl.multiple_of(c * GDN_CHUNK, GDN_CHUNK)
        sbs = [state[j].astype(BF16) for j in range(hb)]
        gs = [_dot(g_ref[j, pl.ds(r0, d), :], sbs[j]) for j in range(hb)]
        os = [_dot(qp_ref[pl.ds(q0, GDN_CHUNK), j * d:(j + 1) * d], sbs[j]) for j in range(hb)]
        for j in range(hb):
            state[j] = eg_ref[j, pl.ds(c, 1), :] * state[j] - gs[j] + n_ref[j, pl.ds(r0, d), :].astype(F32)
            obuf[pl.ds(q0, GDN_CHUNK), j * d:(j + 1) * d] = (
                os[j] + op_ref[pl.ds(q0, GDN_CHUNK), j * d:(j + 1) * d].astype(F32))
        return carry

    lax.fori_loop(0, tr // GDN_CHUNK, chunk_step, 0)

    for j in range(hb):
        cols = slice(j * d, (j + 1) * d)
        o = obuf[:, cols]
        ms = jnp.mean(o * o, axis=-1, keepdims=True)
        o_ref[:, cols] = (o * lax.rsqrt(ms + NORM_EPS) * on_ref[...] * _silu(z_ref[:, cols].astype(F32))
                          ).astype(o_ref.dtype)


def _gdn_recurrence(g_op, n_op, qp, op, eg, proj, z_col0, out_norm, *, hb=8, tr=512):
    n_v, rows, d = g_op.shape
    s = qp.shape[0]
    hb = min(hb, n_v)
    tr = min(tr, s)
    assert n_v % hb == 0 and s % tr == 0 and z_col0 % (hb * d) == 0
    cpb = tr // GDN_CHUNK
    z_blk0 = z_col0 // (hb * d)
    return pl.pallas_call(
        _gdn_rec_kernel,
        out_shape=jax.ShapeDtypeStruct((s, n_v * d), BF16),
        grid=(n_v // hb, s // tr),
        in_specs=[
            pl.BlockSpec((hb, cpb * d, d), lambda h, i: (h, i, 0)),
            pl.BlockSpec((hb, cpb * d, d), lambda h, i: (h, i, 0)),
            pl.BlockSpec((tr, hb * d), lambda h, i: (i, h)),
            pl.BlockSpec((tr, hb * d), lambda h, i: (i, h)),
            pl.BlockSpec((hb, cpb, LANES), lambda h, i: (h, i, 0)),
            pl.BlockSpec((tr, hb * d), lambda h, i: (i, z_blk0 + h)),
            pl.BlockSpec((1, d), lambda h, i: (0, 0)),
        ],
        out_specs=pl.BlockSpec((tr, hb * d), lambda h, i: (i, h)),
        scratch_shapes=[pltpu.VMEM((hb, d, d), F32), pltpu.VMEM((tr, hb * d), F32)],
        compiler_params=_params(("parallel", "arbitrary")),
        name="gdn_recurrence",
    )(g_op, n_op, qp, op, eg, proj, out_norm.reshape(1, d))


def _pad_lanes(vec, offset=0):
    return jnp.zeros((1, LANES), F32).at[0, offset:offset + vec.shape[0]].set(vec.astype(F32))


def kernel(x, norm_mix, norm_mlp, w_mlp_in, w_mlp_out, ab_w_in, ab_conv_w, fox_f_bias, fox_q_norm,
           fox_k_norm, ab_w_out, gdn_w_in, gdn_conv_w, gdn_a_log, gdn_dt_bias, gdn_out_norm, gdn_w_out):
    bsz, s_len, d_model = x.shape
    depth = norm_mix.shape[0]
    width = ab_conv_w.shape[1]
    n_v = gdn_a_log.shape[1]
    qkv = gdn_conv_w.shape[1]
    n_qk = (qkv - n_v * GDN_HEAD_DIM) // (2 * GDN_HEAD_DIM)
    ab_main = 6 * width
    gdn_main = qkv + n_v * GDN_HEAD_DIM
    d_ff = w_mlp_in.shape[2]
    assert ab_w_in.shape[2] == ab_main + FOX_HEADS and gdn_w_in.shape[2] == gdn_main + 2 * n_v
    assert 2 * n_v <= LANES and FOX_HEADS <= LANES

    def side_weight(w_stack, layer, start):
        cols = lax.slice(w_stack, (layer, 0, start), (layer + 1, w_stack.shape[1], w_stack.shape[2]))[0]
        return jnp.pad(cols, ((0, 0), (0, LANES - cols.shape[1])))

    ab_w_main = ab_w_in[:, :, :ab_main].astype(BF16)
    gdn_w_main = gdn_w_in[:, :, :gdn_main].astype(BF16)

    outs = []
    for b in range(bsz):
        xb = x[b]
        for layer in range(depth):
            i = layer // 2
            if layer % 2 == 0:
                proj, fl = _norm_matmul(xb, norm_mix[layer], ab_w_main, i, ab_main,
                                        side_w=side_weight(ab_w_in, i, ab_main), name="ab_in_proj")
                y_a, qn, kn, c_col, c_row, qmax, kmax = _fox_prep(
                    proj, fl, ab_conv_w[i].T, _pad_lanes(fox_f_bias[i]), fox_q_norm[i], fox_k_norm[i], width=width)
                y_b, w_out_b = _fox_attention(qn, kn, proj, 5 * FOX_HEADS, c_col, c_row, qmax, kmax,
                                              (w_mlp_out, layer))
                xb = _matmul_res([y_a, y_b], ab_w_out, i, xb, name="ab_out_proj")
            else:
                proj, side = _norm_matmul(xb, norm_mix[layer], gdn_w_main, i, gdn_main,
                                          side_w=side_weight(gdn_w_in, i, gdn_main), name="gdn_in_proj")
                beta, gc = _gdn_gates(side, _pad_lanes(gdn_a_log[i], n_v), _pad_lanes(gdn_dt_bias[i], n_v))
                g_op, n_op, qp, op, eg, w_out_b = _gdn_prep(proj, beta, gc, gdn_conv_w[i].T, (w_mlp_out, layer),
                                                            n_qk=n_qk, n_v=n_v)
                o = _gdn_recurrence(g_op, n_op, qp, op, eg, proj, qkv, gdn_out_norm[i])
                xb = _matmul_res([o], gdn_w_out, i, xb, name="gdn_out_proj")
            hidden = _norm_matmul(xb, norm_mlp[layer], w_mlp_in, layer, d_ff, act=True, name="mlp_in")
            xb = _matmul_res([hidden], w_out_b, 0, xb, name="mlp_out")
        outs.append(xb)
    return outs[0][None] if bsz == 1 else jnp.stack(outs, axis=0)
```

```python
import functools

import jax
import jax.numpy as jnp
from jax import lax
from jax.experimental import pallas as pl
from jax.experimental.pallas import tpu as pltpu

F32 = jnp.float32
BF16 = jnp.bfloat16

NORM_EPS = 1e-6
FOX_HEADS = 8
CONV_A_K = 3
GDN_HEAD_DIM = 128
GDN_CONV_K = 4
GDN_CHUNK = 64
LANES = 128
HALO = 8
GROUP = 2 * GDN_CHUNK
INV_BASE = 16
VMEM_LIMIT = 60000 * 1024
LOG2E = 1.4426950408889634
PRUNE_LOG2 = 160.0
FAST_LOG2 = 100.0
NORM_MARGIN = 1.001


def _params(semantics, vmem=VMEM_LIMIT):
    return pltpu.CompilerParams(dimension_semantics=semantics, vmem_limit_bytes=vmem)


def _dot(a, b):
    return jnp.dot(a, b, preferred_element_type=F32)


def _dot_nt(a, b):
    return lax.dot_general(a, b, (((1,), (1,)), ((), ())), preferred_element_type=F32)


def _sigmoid(x):
    return 0.5 * jnp.tanh(0.5 * x) + 0.5


def _silu(x):
    return x * _sigmoid(x)


def _softplus(x):
    return jnp.maximum(x, 0.0) + jnp.log1p(jnp.exp(-jnp.abs(x)))


def _norm_matmul_kernel(*refs, act, has_side):
    if has_side:
        x_ref, g_ref, w_ref, ws_ref, o_ref, side_ref, h_ref = refs
    else:
        x_ref, g_ref, w_ref, o_ref, h_ref = refs

    @pl.when(pl.program_id(1) == 0)
    def _():
        xf = x_ref[...]
        ms = jnp.mean(xf * xf, axis=-1, keepdims=True)
        h = (xf * lax.rsqrt(ms + NORM_EPS) * g_ref[...]).astype(BF16)
        h_ref[...] = h
        if has_side:
            side_ref[...] = _dot(h, ws_ref[...].astype(BF16))

    acc = _dot(h_ref[...], w_ref[...].astype(BF16))
    if act:
        acc = jnp.square(jnp.maximum(acc, 0.0))
    o_ref[...] = acc.astype(o_ref.dtype)


STREAMED_W_BYTES = 8 * 1024 * 1024


def _norm_matmul(x, g, w_stack, layer, n, *, act=False, side_w=None, out_dtype=BF16, tm=1024, name):
    m, k = x.shape
    tn = n
    while (k * tn * w_stack.dtype.itemsize > STREAMED_W_BYTES or n % tn) and tn % 256 == 0:
        tn //= 2
    tm = min(tm, m)
    assert m % tm == 0 and n % tn == 0 and w_stack.shape[1] == k
    in_specs = [
        pl.BlockSpec((tm, k), lambda i, j: (i, 0)),
        pl.BlockSpec((1, k), lambda i, j: (0, 0)),
        pl.BlockSpec((None, k, tn), lambda i, j: (layer, 0, j)),
    ]
    args = [x, g.reshape(1, k), w_stack]
    out_shape = [jax.ShapeDtypeStruct((m, n), out_dtype)]
    out_specs = [pl.BlockSpec((tm, tn), lambda i, j: (i, j))]
    if side_w is not None:
        in_specs.append(pl.BlockSpec((k, LANES), lambda i, j: (0, 0)))
        args.append(side_w)
        out_shape.append(jax.ShapeDtypeStruct((m, LANES), F32))
        out_specs.append(pl.BlockSpec((tm, LANES), lambda i, j: (i, 0)))
    res = pl.pallas_call(
        functools.partial(_norm_matmul_kernel, act=act, has_side=side_w is not None),
        out_shape=out_shape,
        grid=(m // tm, n // tn),
        in_specs=in_specs,
        out_specs=out_specs,
        scratch_shapes=[pltpu.VMEM((tm, k), BF16)],
        compiler_params=_params(("parallel", "arbitrary")),
        name=name,
    )(*args)
    return res if side_w is not None else res[0]


def _cast_through_specs(stack, layer, grid):
    rows, cols = stack.shape[1:]
    steps = grid[0] * grid[1]
    slab = rows // steps
    assert slab * steps == rows and slab % 16 == 0
    step = lambda a, b: a * grid[1] + b
    in_spec = pl.BlockSpec((None, slab, cols), lambda a, b: (layer, step(a, b), 0))
    out_spec = pl.BlockSpec((None, slab, cols), lambda a, b: (0, step(a, b), 0))
    return in_spec, out_spec, jax.ShapeDtypeStruct((1, rows, cols), BF16)


def _matmul_res_kernel(*refs, n_a, cast_w):
    a_refs = refs[:n_a]
    if cast_w:
        w_ref, r_ref, o_ref, wb_ref = refs[n_a:]

        @pl.when(pl.program_id(1) == 0)
        def _():
            wb_ref[...] = w_ref[...].astype(BF16)
    else:
        wb_ref, r_ref, o_ref = refs[n_a:]

    acc = r_ref[...]
    off = 0
    for a_ref in a_refs:
        ka = a_ref.shape[1]
        acc = acc + _dot(a_ref[...], wb_ref[off:off + ka, :])
        off += ka
    o_ref[...] = acc


RESIDENT_W_BYTES = 32 * 1024 * 1024
PIPELINE_MARGIN_BYTES = 4 * 1024 * 1024
MAX_ROW_BLOCK = 512


def _matmul_res(a_list, w_stack, layer, res, *, name):
    m = res.shape[0]
    k, n = w_stack.shape[1:]
    cast_w = w_stack.dtype != BF16
    w_bytes = w_stack.dtype.itemsize + (2 if cast_w else 0)
    tn = n
    while k * tn * w_bytes > RESIDENT_W_BYTES and tn % 2 == 0:
        tn //= 2
    row_bytes = 2 * k * 2 + 4 * tn * 4
    tm = min(MAX_ROW_BLOCK, m)
    while tm * row_bytes > VMEM_LIMIT - k * tn * w_bytes - PIPELINE_MARGIN_BYTES and tm % 2 == 0:
        tm //= 2
    assert m % tm == 0 and n % tn == 0 and sum(a.shape[1] for a in a_list) == k
    return pl.pallas_call(
        functools.partial(_matmul_res_kernel, n_a=len(a_list), cast_w=cast_w),
        out_shape=jax.ShapeDtypeStruct((m, n), F32),
        grid=(n // tn, m // tm),
        in_specs=[pl.BlockSpec((tm, a.shape[1]), lambda j, i: (i, 0)) for a in a_list] + [
            pl.BlockSpec((None, k, tn), lambda j, i: (layer, 0, j), pipeline_mode=pl.Buffered(1)),
            pl.BlockSpec((tm, tn), lambda j, i: (i, j)),
        ],
        out_specs=pl.BlockSpec((tm, tn), lambda j, i: (i, j)),
        scratch_shapes=[pltpu.VMEM((k, tn), BF16)] if cast_w else [],
        compiler_params=_params(("parallel", "arbitrary")),
        name=name,
    )(*a_list, w_stack, res)


def _fox_prep_kernel(gb_ref, gc_ref, xv_ref, q_ref, k_ref, fl_ref, cw_ref, fb_ref, qg_ref, kg_ref,
                     ya_ref, qn_ref, kn_ref, c_ref, ct_ref, qmax_ref, kmax_ref, ubuf, carry, *, head_dim, scale):
    tb = gb_ref.shape[0]
    first = pl.program_id(0) == 0

    @pl.when(first)
    def _():
        ubuf[0:HALO, :] = jnp.zeros((HALO, ubuf.shape[1]), F32)
        carry[...] = jnp.zeros_like(carry)
        qmax_ref[...] = jnp.zeros_like(qmax_ref)
        kmax_ref[...] = jnp.zeros_like(kmax_ref)

    ubuf[HALO:HALO + tb, :] = gc_ref[...].astype(F32) * xv_ref[...].astype(F32)
    y = jnp.zeros((tb, ubuf.shape[1]), F32)
    for j in range(CONV_A_K):
        y = y + cw_ref[j:j + 1, :] * ubuf[pl.ds(HALO - (CONV_A_K - 1) + j, tb), :]
    ya_ref[...] = (gb_ref[...].astype(F32) * y).astype(ya_ref.dtype)
    ubuf[0:HALO, :] = ubuf[tb:tb + HALO, :]

    for h in range(q_ref.shape[1] // head_dim):
        sl = slice(h * head_dim, (h + 1) * head_dim)
        qh = q_ref[:, sl].astype(F32)
        kh = k_ref[:, sl].astype(F32)
        qms = jnp.mean(qh * qh, axis=-1, keepdims=True)
        kms = jnp.mean(kh * kh, axis=-1, keepdims=True)
        qn = (qh * lax.rsqrt(qms + NORM_EPS) * (qg_ref[...] * scale)).astype(qn_ref.dtype)
        kn = (kh * lax.rsqrt(kms + NORM_EPS) * kg_ref[...]).astype(kn_ref.dtype)
        qn_ref[:, sl] = qn
        kn_ref[:, sl] = kn
        qf, kf = qn.astype(F32), kn.astype(F32)
        q_norm = jnp.sqrt(jnp.max(jnp.sum(qf * qf, axis=-1, keepdims=True), axis=0, keepdims=True))
        k_norm = jnp.sqrt(jnp.max(jnp.sum(kf * kf, axis=-1, keepdims=True), axis=0, keepdims=True))
        qmax_ref[h:h + 1, :] = jnp.maximum(qmax_ref[h:h + 1, :], q_norm)
        kmax_ref[h:h + 1, :] = jnp.maximum(kmax_ref[h:h + 1, :], k_norm)

    z = fl_ref[...] + fb_ref[...]
    log_f = jnp.minimum(z, 0.0) - jnp.log1p(jnp.exp(-jnp.abs(z)))
    row = lax.broadcasted_iota(jnp.int32, (tb, tb), 0)
    col = lax.broadcasted_iota(jnp.int32, (tb, tb), 1)
    tri = (col <= row).astype(F32)
    c = jnp.dot(tri, log_f, precision=lax.Precision.HIGHEST, preferred_element_type=F32) + carry[...]
    c_ref[...] = c
    ct_ref[...] = c.T[0:ct_ref.shape[0], :]
    carry[...] = c[tb - 1:tb, :]


def _fox_prep(proj, fl, conv_w_t, f_bias_p, q_norm, k_norm, *, width, tb=512):
    s = proj.shape[0]
    tb = min(tb, s)
    head_dim = width // FOX_HEADS
    col = lambda c: pl.BlockSpec((tb, width), lambda i, c=c: (i, c))
    const = lambda shape: pl.BlockSpec(shape, lambda i: (0, 0))
    return pl.pallas_call(
        functools.partial(_fox_prep_kernel, head_dim=head_dim, scale=LOG2E * head_dim ** -0.5),
        out_shape=[
            jax.ShapeDtypeStruct((s, width), BF16),
            jax.ShapeDtypeStruct((s, width), BF16),
            jax.ShapeDtypeStruct((s, width), BF16),
            jax.ShapeDtypeStruct((s, LANES), F32),
            jax.ShapeDtypeStruct((FOX_HEADS, s), F32),
            jax.ShapeDtypeStruct((FOX_HEADS, LANES), F32),
            jax.ShapeDtypeStruct((FOX_HEADS, LANES), F32),
        ],
        grid=(s // tb,),
        in_specs=[col(0), col(1), col(2), col(3), col(4),
                  pl.BlockSpec((tb, LANES), lambda i: (i, 0)),
                  const((CONV_A_K, width)), const((1, LANES)),
                  const((1, head_dim)), const((1, head_dim))],
        out_specs=[
            pl.BlockSpec((tb, width), lambda i: (i, 0)),
            pl.BlockSpec((tb, width), lambda i: (i, 0)),
            pl.BlockSpec((tb, width), lambda i: (i, 0)),
            pl.BlockSpec((tb, LANES), lambda i: (i, 0)),
            pl.BlockSpec((FOX_HEADS, tb), lambda i: (0, i)),
            const((FOX_HEADS, LANES)), const((FOX_HEADS, LANES)),
        ],
        scratch_shapes=[pltpu.VMEM((tb + HALO, width), F32), pltpu.VMEM((1, LANES), F32)],
        compiler_params=_params(("arbitrary",)),
        name="fox_prep",
    )(proj, proj, proj, proj, proj, fl, conv_w_t, f_bias_p, q_norm.reshape(1, -1), k_norm.reshape(1, -1))


def _fox_attn_kernel(cend_ref, cstart_ref, bound_ref, q_ref, k_ref, v_ref, cq_ref, ck_ref, *rest, tile, n_cast):
    wf_refs, o_ref, wb_refs = rest[:n_cast], rest[n_cast], rest[n_cast + 1:2 * n_cast + 1]
    m_ref, l_ref, acc_ref = rest[2 * n_cast + 1:]
    for wf_ref, wb_ref in zip(wf_refs, wb_refs):
        wb_ref[...] = wf_ref[...].astype(BF16)
    h = pl.program_id(0)
    qi = pl.program_id(1)
    reps = tile // LANES
    lane = lax.broadcasted_iota(jnp.int32, (tile, LANES), 1)
    cq = jnp.sum(jnp.where(lane == h, cq_ref[...], 0.0), axis=-1, keepdims=True) * LOG2E
    cq = jnp.broadcast_to(cq, (tile, LANES))
    q = q_ref[...]

    def scores(j):
        start = pl.multiple_of(j * tile, tile)
        ck = ck_ref[h, pl.ds(j, 1), :] * LOG2E
        return _dot_nt(q, k_ref[pl.ds(start, tile), :]) - ck, start

    def accumulate(s, start, stab):
        p = jnp.exp2(s - jnp.concatenate([stab] * reps, axis=1))
        part = p[:, 0:LANES]
        for r in range(1, reps):
            part = part + p[:, r * LANES:(r + 1) * LANES]
        l_ref[...] += part
        acc_ref[...] += _dot(p.astype(v_ref.dtype), v_ref[pl.ds(start, tile), :])

    def online_step(j, masked):
        s, start = scores(j)
        if masked:
            q_pos = lax.broadcasted_iota(jnp.int32, (tile, tile), 0)
            k_pos = lax.broadcasted_iota(jnp.int32, (tile, tile), 1)
            s = jnp.where(k_pos <= q_pos, s, -jnp.inf)
        m_old = m_ref[...]
        m_new = jnp.maximum(m_old, jnp.broadcast_to(jnp.max(s, axis=-1, keepdims=True), (tile, LANES)) + cq)
        alpha = jnp.exp2(m_old - m_new)
        l_ref[...] = alpha * l_ref[...]
        acc_ref[...] = alpha * acc_ref[...]
        m_ref[...] = m_new
        accumulate(s, start, m_new - cq)

    m_ref[...] = jnp.full_like(m_ref, -jnp.inf)
    l_ref[...] = jnp.zeros_like(l_ref)
    acc_ref[...] = jnp.zeros_like(acc_ref)
    online_step(qi, True)

    two_b = 2.0 * bound_ref[h]
    reach = two_b + PRUNE_LOG2
    c_tile = cstart_ref[h, qi]

    def in_reach(j):
        return jnp.logical_and(j >= 0, cend_ref[h, jnp.maximum(j, 0)] - c_tile <= reach)

    def fast_block(j):
        s, start = scores(j)
        accumulate(s, start, m_ref[...] - cq)
        return j - 1

    def online_block(j):
        online_step(j, False)
        return j - 1

    @pl.when(two_b <= FAST_LOG2)
    def _():
        lax.while_loop(in_reach, fast_block, qi - 1)

    @pl.when(two_b > FAST_LOG2)
    def _():
        lax.while_loop(in_reach, online_block, qi - 1)

    o_ref[...] = (acc_ref[...] / jnp.sum(l_ref[...], axis=-1, keepdims=True)).astype(o_ref.dtype)


def _fox_attention(qn, kn, proj, v_col0, c_col, c_row, qmax, kmax, casts, *, tile=512):
    s, width = qn.shape
    d = width // FOX_HEADS
    tile = min(tile, s)
    assert d == LANES and s % tile == 0 and tile % LANES == 0
    n_blk = s // tile
    cast_specs = [_cast_through_specs(*cast, (FOX_HEADS, n_blk)) for cast in casts]
    cend = c_row[:, tile - 1::tile] * LOG2E
    cstart = c_row[:, ::tile] * LOG2E
    bound = qmax[:, 0] * kmax[:, 0] * NORM_MARGIN
    smem = pl.BlockSpec(memory_space=pltpu.SMEM)
    return pl.pallas_call(
        functools.partial(_fox_attn_kernel, tile=tile, n_cast=len(casts)),
        out_shape=[jax.ShapeDtypeStruct((s, width), BF16)] + [spec[2] for spec in cast_specs],
        grid=(FOX_HEADS, n_blk),
        in_specs=[
            smem, smem, smem,
            pl.BlockSpec((tile, d), lambda h, qi: (qi, h)),
            pl.BlockSpec((s, d), lambda h, qi: (0, h)),
            pl.BlockSpec((s, d), lambda h, qi: (0, v_col0 + h)),
            pl.BlockSpec((tile, LANES), lambda h, qi: (qi, 0)),
            pl.BlockSpec((FOX_HEADS, n_blk, tile), lambda h, qi: (0, 0, 0)),
        ] + [spec[0] for spec in cast_specs],
        out_specs=[pl.BlockSpec((tile, d), lambda h, qi: (qi, h))] + [spec[1] for spec in cast_specs],
        scratch_shapes=[pltpu.VMEM((tile, LANES), F32), pltpu.VMEM((tile, LANES), F32),
                        pltpu.VMEM((tile, d), F32)],
        compiler_params=_params(("parallel", "arbitrary")),
        name="fox_attention",
    )(cend, cstart, bound, qn, kn, proj, c_col, c_row.reshape(FOX_HEADS, n_blk, tile),
      *[cast[0] for cast in casts])


def _gdn_gates_kernel(side_ref, alog_ref, dtb_ref, beta_ref, gc_ref):
    side = side_ref[...]
    tb = side.shape[0]
    beta_ref[...] = _sigmoid(side)
    g = -jnp.exp(alog_ref[...]) * _softplus(side + dtb_ref[...])
    row = lax.broadcasted_iota(jnp.int32, (tb, tb), 0)
    col = lax.broadcasted_iota(jnp.int32, (tb, tb), 1)
    tri = jnp.logical_and(row // GDN_CHUNK == col // GDN_CHUNK, col <= row).astype(F32)
    gc_ref[...] = jnp.dot(tri, g, precision=lax.Precision.HIGHEST, preferred_element_type=F32)


def _gdn_gates(side, alog_p, dtb_p, *, tb=512):
    s = side.shape[0]
    tb = min(tb, s)
    assert s % tb == 0 and tb % GDN_CHUNK == 0
    blk = pl.BlockSpec((tb, LANES), lambda i: (i, 0))
    const = pl.BlockSpec((1, LANES), lambda i: (0, 0))
    return pl.pallas_call(
        _gdn_gates_kernel,
        out_shape=[jax.ShapeDtypeStruct((s, LANES), F32), jax.ShapeDtypeStruct((s, LANES), F32)],
        grid=(s // tb,),
        in_specs=[blk, const, const],
        out_specs=[blk, blk],
        compiler_params=_params(("parallel",)),
        name="gdn_gates",
    )(side, alog_p, dtb_p)


def _conv_silu(buf, raw_ref, w_ref, tc):
    buf[HALO:HALO + tc, :] = raw_ref[...].astype(F32)
    y = jnp.zeros((tc, buf.shape[1]), F32)
    for j in range(GDN_CONV_K):
        y = y + w_ref[j:j + 1, :] * buf[pl.ds(HALO - (GDN_CONV_K - 1) + j, tc), :]
    buf[0:HALO, :] = buf[tc:tc + HALO, :]
    return _silu(y)


def _l2_norm(x):
    return x * lax.rsqrt(jnp.sum(x * x, axis=-1, keepdims=True) + NORM_EPS)


def _unit_lower_inverses(l_mats, eye, row, col):
    n = eye.shape[0]
    same_base = (row // INV_BASE) == (col // INV_BASE)
    lbs = [jnp.where(same_base, l, 0.0).astype(BF16) for l in l_mats]
    ps = [eye - lb.astype(F32) for lb in lbs]
    ms = [_dot(lb, lb) for lb in lbs]
    power = 2
    while power < INV_BASE:
        mbs = [m.astype(BF16) for m in ms]
        if 2 * power < INV_BASE:
            boths = [_dot(mb, jnp.concatenate([mb, p.astype(BF16)], axis=1)) for mb, p in zip(mbs, ps)]
            ms = [b[:, :n] for b in boths]
            xs = [b[:, n:] for b in boths]
        else:
            xs = [_dot(mb, p.astype(BF16)) for mb, p in zip(mbs, ps)]
        ps = [p + x for p, x in zip(ps, xs)]
        power *= 2
    size = INV_BASE
    while size < GDN_CHUNK:
        below = jnp.logical_and(row // (2 * size) == col // (2 * size), row // size == col // size + 1)
        c_offs = [jnp.where(below, l, 0.0).astype(BF16) for l in l_mats]
        pbs = [p.astype(BF16) for p in ps]
        ts = [_dot(c, pb).astype(BF16) for c, pb in zip(c_offs, pbs)]
        ps = [p - _dot(pb, t) for p, pb, t in zip(ps, pbs, ts)]
        size *= 2
    return ps


def _gdn_prep_kernel(q_ref, k_ref, v_ref, beta_ref, gc_ref, wq_ref, wk_ref, wv_ref, wf_ref,
                     g_out, n_out, qp_out, op_out, eg_out, wb_ref, qbuf, kbuf, vbuf, *, n_vheads):
    wb_ref[...] = wf_ref[...].astype(BF16)
    tc = q_ref.shape[0]
    d = GDN_HEAD_DIM
    hq = pl.program_id(0)
    rep = v_ref.shape[1] // d
    chunks_per_group = GROUP // GDN_CHUNK

    @pl.when(pl.program_id(1) == 0)
    def _():
        qbuf[0:HALO, :] = jnp.zeros((HALO, qbuf.shape[1]), F32)
        kbuf[0:HALO, :] = jnp.zeros((HALO, kbuf.shape[1]), F32)
        vbuf[0:HALO, :] = jnp.zeros((HALO, vbuf.shape[1]), F32)

    q = _l2_norm(_conv_silu(qbuf, q_ref, wq_ref, tc)) * (d ** -0.5)
    k = _l2_norm(_conv_silu(kbuf, k_ref, wk_ref, tc))
    v = _conv_silu(vbuf, v_ref, wv_ref, tc)

    beta_all = beta_ref[...]
    gc_all = gc_ref[...]
    lane = lax.broadcasted_iota(jnp.int32, beta_all.shape, 1)
    row = lax.broadcasted_iota(jnp.int32, (GROUP, GROUP), 0)
    col = lax.broadcasted_iota(jnp.int32, (GROUP, GROUP), 1)
    same_chunk = (row // GDN_CHUNK) == (col // GDN_CHUNK)
    incl = jnp.logical_and(same_chunk, col <= row)
    strict = jnp.logical_and(same_chunk, col < row)
    eye = (row == col).astype(F32)

    heads = []
    for j in range(rep):
        vh = hq * rep + j
        beta = jnp.sum(jnp.where(lane == vh, beta_all, 0.0), axis=-1, keepdims=True)
        gc = jnp.sum(jnp.where(lane == n_vheads + vh, gc_all, 0.0), axis=-1, keepdims=True)
        heads.append((beta, gc))

    probs = [(j, r) for j in range(rep) for r in range(tc // GROUP)]
    rows_of = lambda r: slice(r * GROUP, (r + 1) * GROUP)
    kgs = [k[rows_of(r)] for _, r in probs]
    qgs = [q[rows_of(r)] for _, r in probs]
    kgbs = [kg.astype(BF16) for kg in kgs]
    bgs = [heads[j][0][rows_of(r)] for j, r in probs]
    gcbs = [jnp.broadcast_to(heads[j][1][rows_of(r)], (GROUP, GROUP)) for j, r in probs]
    gcrs = [gcb.T for gcb in gcbs]
    decays = [jnp.where(incl, jnp.exp(gcb - gcr), 0.0) for gcb, gcr in zip(gcbs, gcrs)]
    kbs = [kg * bg for kg, bg in zip(kgs, bgs)]
    kqs = [_dot_nt(jnp.concatenate([kb, qg], axis=0).astype(BF16), kgb) for kb, qg, kgb in zip(kbs, qgs, kgbs)]
    l_mats = [jnp.where(strict, kq[:GROUP] * dec, 0.0) for kq, dec in zip(kqs, decays)]
    scores = [(kq[GROUP:] * dec).astype(BF16) for kq, dec in zip(kqs, decays)]
    t_invs = _unit_lower_inverses(l_mats, eye, row, col)
    egcs = [jnp.exp(gcb) for gcb in gcbs]
    rhs = [jnp.concatenate([(v[rows_of(r), j * d:(j + 1) * d] * bg).astype(BF16), (kb * egc).astype(BF16)], axis=1)
           for (j, r), bg, kb, egc in zip(probs, bgs, kbs, egcs)]
    uws = [_dot(t.astype(BF16), x) for t, x in zip(t_invs, rhs)]
    wus = [jnp.concatenate([uw[:, d:], uw[:, :d]], axis=1).astype(BF16) for uw in uws]
    g_lasts = [jnp.where(row < GDN_CHUNK, gcb[GDN_CHUNK - 1:GDN_CHUNK, :], gcb[GROUP - 1:GROUP, :]) for gcb in gcbs]
    kdts = [(kg * jnp.exp(gl - gcb)).T for kg, gl, gcb in zip(kgs, g_lasts, gcbs)]
    qos = [_dot(sc, wu) for sc, wu in zip(scores, wus)]
    gns = [[_dot(jnp.where(col // GDN_CHUNK == c, kdt, 0.0).astype(BF16), wu) for c in range(chunks_per_group)]
           for kdt, wu in zip(kdts, wus)]

    for p, (j, r) in enumerate(probs):
        cols = slice(j * d, (j + 1) * d)
        qp_out[rows_of(r), cols] = (qgs[p] * egcs[p] - qos[p][:, :d]).astype(qp_out.dtype)
        op_out[rows_of(r), cols] = qos[p][:, d:].astype(op_out.dtype)
        for c in range(chunks_per_group):
            chunk = r * chunks_per_group + c
            g_out[j, chunk * d:(chunk + 1) * d, :] = gns[p][c][:, :d].astype(g_out.dtype)
            n_out[j, chunk * d:(chunk + 1) * d, :] = gns[p][c][:, d:].astype(n_out.dtype)
            last = (c + 1) * GDN_CHUNK - 1
            eg_out[j, chunk:chunk + 1, :] = jnp.exp(gcbs[p][last:last + 1, :])


def _gdn_prep(proj, beta, gc, conv_w_t, cast, *, n_qk, n_v, tc=512):
    s = proj.shape[0]
    d = GDN_HEAD_DIM
    rep = n_v // n_qk
    tc = min(tc, s)
    assert s % tc == 0 and tc % GROUP == 0
    kw = n_qk * d
    v_blk0 = 2 * kw // (rep * d)
    n_chunks = s // GDN_CHUNK
    cpb = tc // GDN_CHUNK
    cast_in, cast_out, cast_shape = _cast_through_specs(*cast, (n_qk, s // tc))
    return pl.pallas_call(
        functools.partial(_gdn_prep_kernel, n_vheads=n_v),
        out_shape=[
            jax.ShapeDtypeStruct((n_v, n_chunks * d, d), BF16),
            jax.ShapeDtypeStruct((n_v, n_chunks * d, d), BF16),
            jax.ShapeDtypeStruct((s, n_v * d), BF16),
            jax.ShapeDtypeStruct((s, n_v * d), BF16),
            jax.ShapeDtypeStruct((n_v, n_chunks, LANES), F32),
            cast_shape,
        ],
        grid=(n_qk, s // tc),
        in_specs=[
            pl.BlockSpec((tc, d), lambda h, i: (i, h)),
            pl.BlockSpec((tc, d), lambda h, i: (i, n_qk + h)),
            pl.BlockSpec((tc, rep * d), lambda h, i: (i, v_blk0 + h)),
            pl.BlockSpec((tc, LANES), lambda h, i: (i, 0)),
            pl.BlockSpec((tc, LANES), lambda h, i: (i, 0)),
            pl.BlockSpec((GDN_CONV_K, d), lambda h, i: (0, h)),
            pl.BlockSpec((GDN_CONV_K, d), lambda h, i: (0, n_qk + h)),
            pl.BlockSpec((GDN_CONV_K, rep * d), lambda h, i: (0, v_blk0 + h)),
            cast_in,
        ],
        out_specs=[
            pl.BlockSpec((rep, cpb * d, d), lambda h, i: (h, i, 0)),
            pl.BlockSpec((rep, cpb * d, d), lambda h, i: (h, i, 0)),
            pl.BlockSpec((tc, rep * d), lambda h, i: (i, h)),
            pl.BlockSpec((tc, rep * d), lambda h, i: (i, h)),
            pl.BlockSpec((rep, cpb, LANES), lambda h, i: (h, i, 0)),
            cast_out,
        ],
        scratch_shapes=[pltpu.VMEM((tc + HALO, d), F32), pltpu.VMEM((tc + HALO, d), F32),
                        pltpu.VMEM((tc + HALO, rep * d), F32)],
        compiler_params=_params(("parallel", "arbitrary")),
        name="gdn_prep",
    )(proj, proj, proj, beta, gc, conv_w_t, conv_w_t, conv_w_t, cast[0])


def _gdn_rec_kernel(g_ref, n_ref, qp_ref, op_ref, eg_ref, z_ref, on_ref, o_ref, state, obuf):
    hb = g_ref.shape[0]
    tr = qp_ref.shape[0]
    d = GDN_HEAD_DIM

    @pl.when(pl.program_id(1) == 0)
    def _():
        state[...] = jnp.zeros_like(state)

    def chunk_step(c, carry):
        r0 = pl.multiple_of(c * d, d)
        q0 = pl.multiple_of(c * GDN_CHUNK, GDN_CHUNK)
        sbs = [state[j].astype(BF16) for j in range(hb)]
        gs = [_dot(g_ref[j, pl.ds(r0, d), :], sbs[j]) for j in range(hb)]
        os = [_dot(qp_ref[pl.ds(q0, GDN_CHUNK), j * d:(j + 1) * d], sbs[j]) for j in range(hb)]
        for j in range(hb):
            state[j] = eg_ref[j, pl.ds(c, 1), :] * state[j] - gs[j] + n_ref[j, pl.ds(r0, d), :].astype(F32)
            obuf[pl.ds(q0, GDN_CHUNK), j * d:(j + 1) * d] = (
                os[j] + op_ref[pl.ds(q0, GDN_CHUNK), j * d:(j + 1) * d].astype(F32))
        return carry

    lax.fori_loop(0, tr // GDN_CHUNK, chunk_step, 0)

    for j in range(hb):
        cols = slice(j * d, (j + 1) * d)
        o = obuf[:, cols]
        ms = jnp.mean(o * o, axis=-1, keepdims=True)
        o_ref[:, cols] = (o * lax.rsqrt(ms + NORM_EPS) * on_ref[...] * _silu(z_ref[:, cols].astype(F32))
                          ).astype(o_ref.dtype)


def _gdn_recurrence(g_op, n_op, qp, op, eg, proj, z_col0, out_norm, *, hb=8, tr=512):
    n_v, rows, d = g_op.shape
    s = qp.shape[0]
    hb = min(hb, n_v)
    tr = min(tr, s)
    assert n_v % hb == 0 and s % tr == 0 and z_col0 % (hb * d) == 0
    cpb = tr // GDN_CHUNK
    z_blk0 = z_col0 // (hb * d)
    return pl.pallas_call(
        _gdn_rec_kernel,
        out_shape=jax.ShapeDtypeStruct((s, n_v * d), BF16),
        grid=(n_v // hb, s // tr),
        in_specs=[
            pl.BlockSpec((hb, cpb * d, d), lambda h, i: (h, i, 0)),
            pl.BlockSpec((hb, cpb * d, d), lambda h, i: (h, i, 0)),
            pl.BlockSpec((tr, hb * d), lambda h, i: (i, h)),
            pl.BlockSpec((tr, hb * d), lambda h, i: (i, h)),
            pl.BlockSpec((hb, cpb, LANES), lambda h, i: (h, i, 0)),
            pl.BlockSpec((tr, hb * d), lambda h, i: (i, z_blk0 + h)),
            pl.BlockSpec((1, d), lambda h, i: (0, 0)),
        ],
        out_specs=pl.BlockSpec((tr, hb * d), lambda h, i: (i, h)),
        scratch_shapes=[pltpu.VMEM((hb, d, d), F32), pltpu.VMEM((tr, hb * d), F32)],
        compiler_params=_params(("parallel", "arbitrary")),
        name="gdn_recurrence",
    )(g_op, n_op, qp, op, eg, proj, out_norm.reshape(1, d))


def _pad_lanes(vec, offset=0):
    return jnp.zeros((1, LANES), F32).at[0, offset:offset + vec.shape[0]].set(vec.astype(F32))


def kernel(x, norm_mix, norm_mlp, w_mlp_in, w_mlp_out, ab_w_in, ab_conv_w, fox_f_bias, fox_q_norm,
           fox_k_norm, ab_w_out, gdn_w_in, gdn_conv_w, gdn_a_log, gdn_dt_bias, gdn_out_norm, gdn_w_out):
    bsz, s_len, d_model = x.shape
    depth = norm_mix.shape[0]
    width = ab_conv_w.shape[1]
    n_v = gdn_a_log.shape[1]
    qkv = gdn_conv_w.shape[1]
    n_qk = (qkv - n_v * GDN_HEAD_DIM) // (2 * GDN_HEAD_DIM)
    ab_main = 6 * width
    gdn_main = qkv + n_v * GDN_HEAD_DIM
    d_ff = w_mlp_in.shape[2]
    assert ab_w_in.shape[2] == ab_main + FOX_HEADS and gdn_w_in.shape[2] == gdn_main + 2 * n_v
    assert 2 * n_v <= LANES and FOX_HEADS <= LANES

    def side_weight(w_stack, layer, start):
        cols = lax.slice(w_stack, (layer, 0, start), (layer + 1, w_stack.shape[1], w_stack.shape[2]))[0]
        return jnp.pad(cols, ((0, 0), (0, LANES - cols.shape[1])))

    ab_w_main = ab_w_in[:, :, :ab_main].astype(BF16)
    gdn_w_main = gdn_w_in[:, :, :gdn_main]

    outs = []
    for b in range(bsz):
        xb = x[b]
        for layer in range(depth):
            i = layer // 2
            if layer % 2 == 0:
                proj, fl = _norm_matmul(xb, norm_mix[layer], ab_w_main, i, ab_main,
                                        side_w=side_weight(ab_w_in, i, ab_main), name="ab_in_proj")
                y_a, qn, kn, c_col, c_row, qmax, kmax = _fox_prep(
                    proj, fl, ab_conv_w[i].T, _pad_lanes(fox_f_bias[i]), fox_q_norm[i], fox_k_norm[i], width=width)
                casts = [(w_mlp_out, layer)] + ([(gdn_w_main, i)] if layer + 1 < depth else [])
                y_b, w_out_b, *gdn_w_b = _fox_attention(qn, kn, proj, 5 * FOX_HEADS, c_col, c_row, qmax, kmax, casts)
                xb = _matmul_res([y_a, y_b], ab_w_out, i, xb, name="ab_out_proj")
            else:
                proj, side = _norm_matmul(xb, norm_mix[layer], gdn_w_b[0], 0, gdn_main,
                                          side_w=side_weight(gdn_w_in, i, gdn_main), name="gdn_in_proj")
                beta, gc = _gdn_gates(side, _pad_lanes(gdn_a_log[i], n_v), _pad_lanes(gdn_dt_bias[i], n_v))
                g_op, n_op, qp, op, eg, w_out_b = _gdn_prep(proj, beta, gc, gdn_conv_w[i].T, (w_mlp_out, layer),
                                                            n_qk=n_qk, n_v=n_v)
                o = _gdn_recurrence(g_op, n_op, qp, op, eg, proj, qkv, gdn_out_norm[i])
                xb = _matmul_res([o], gdn_w_out, i, xb, name="gdn_out_proj")
            hidden = _norm_matmul(xb, norm_mlp[layer], w_mlp_in, layer, d_ff, act=True, name="mlp_in")
            xb = _matmul_res([hidden], w_out_b, 0, xb, name="mlp_out")
        outs.append(xb)
    return outs[0][None] if bsz == 1 else jnp.stack(outs, axis=0)
```

```python
import functools

import jax
import jax.numpy as jnp
from jax import lax
from jax.experimental import pallas as pl
from jax.experimental.pallas import tpu as pltpu

F32 = jnp.float32
BF16 = jnp.bfloat16

NORM_EPS = 1e-6
FOX_HEADS = 8
CONV_A_K = 3
GDN_HEAD_DIM = 128
GDN_CONV_K = 4
GDN_CHUNK = 64
LANES = 128
HALO = 8
GROUP = 2 * GDN_CHUNK
INV_BASE = 16
VMEM_LIMIT = 60000 * 1024
LOG2E = 1.4426950408889634
PRUNE_LOG2 = 160.0
FAST_LOG2 = 100.0
NORM_MARGIN = 1.001


def _params(semantics, vmem=VMEM_LIMIT):
    return pltpu.CompilerParams(dimension_semantics=semantics, vmem_limit_bytes=vmem)


def _dot(a, b):
    return jnp.dot(a, b, preferred_element_type=F32)


def _dot_nt(a, b):
    return lax.dot_general(a, b, (((1,), (1,)), ((), ())), preferred_element_type=F32)


def _sigmoid(x):
    return 0.5 * jnp.tanh(0.5 * x) + 0.5


def _silu(x):
    return x * _sigmoid(x)


def _softplus(x):
    return jnp.maximum(x, 0.0) + jnp.log1p(jnp.exp(-jnp.abs(x)))


def _norm_matmul_kernel(*refs, act, has_side):
    if has_side:
        x_ref, g_ref, w_ref, ws_ref, o_ref, side_ref, h_ref = refs
    else:
        x_ref, g_ref, w_ref, o_ref, h_ref = refs

    @pl.when(pl.program_id(1) == 0)
    def _():
        xf = x_ref[...]
        ms = jnp.mean(xf * xf, axis=-1, keepdims=True)
        h = (xf * lax.rsqrt(ms + NORM_EPS) * g_ref[...]).astype(BF16)
        h_ref[...] = h
        if has_side:
            side_ref[...] = _dot(h, ws_ref[...].astype(BF16))

    acc = _dot(h_ref[...], w_ref[...].astype(BF16))
    if act:
        acc = jnp.square(jnp.maximum(acc, 0.0))
    o_ref[...] = acc.astype(o_ref.dtype)


STREAMED_W_BYTES = 8 * 1024 * 1024


def _norm_matmul(x, g, w_stack, layer, n, *, act=False, side_w=None, out_dtype=BF16, tm=1024, name):
    m, k = x.shape
    tn = n
    while (k * tn * w_stack.dtype.itemsize > STREAMED_W_BYTES or n % tn) and tn % 256 == 0:
        tn //= 2
    tm = min(tm, m)
    assert m % tm == 0 and n % tn == 0 and w_stack.shape[1] == k
    in_specs = [
        pl.BlockSpec((tm, k), lambda i, j: (i, 0)),
        pl.BlockSpec((1, k), lambda i, j: (0, 0)),
        pl.BlockSpec((None, k, tn), lambda i, j: (layer, 0, j)),
    ]
    args = [x, g.reshape(1, k), w_stack]
    out_shape = [jax.ShapeDtypeStruct((m, n), out_dtype)]
    out_specs = [pl.BlockSpec((tm, tn), lambda i, j: (i, j))]
    if side_w is not None:
        in_specs.append(pl.BlockSpec((k, LANES), lambda i, j: (0, 0)))
        args.append(side_w)
        out_shape.append(jax.ShapeDtypeStruct((m, LANES), F32))
        out_specs.append(pl.BlockSpec((tm, LANES), lambda i, j: (i, 0)))
    res = pl.pallas_call(
        functools.partial(_norm_matmul_kernel, act=act, has_side=side_w is not None),
        out_shape=out_shape,
        grid=(m // tm, n // tn),
        in_specs=in_specs,
        out_specs=out_specs,
        scratch_shapes=[pltpu.VMEM((tm, k), BF16)],
        compiler_params=_params(("parallel", "arbitrary")),
        name=name,
    )(*args)
    return res if side_w is not None else res[0]


def _cast_through_specs(stack, layer, grid):
    rows, cols = stack.shape[1:]
    steps = grid[0] * grid[1]
    slab = rows // steps
    assert slab * steps == rows and slab % 16 == 0
    step = lambda a, b: a * grid[1] + b
    in_spec = pl.BlockSpec((None, slab, cols), lambda a, b: (layer, step(a, b), 0))
    out_spec = pl.BlockSpec((None, slab, cols), lambda a, b: (0, step(a, b), 0))
    return in_spec, out_spec, jax.ShapeDtypeStruct((1, rows, cols), BF16)


def _matmul_res_kernel(*refs, n_a, cast_w):
    a_refs = refs[:n_a]
    if cast_w:
        w_ref, r_ref, o_ref, wb_ref = refs[n_a:]

        @pl.when(pl.program_id(1) == 0)
        def _():
            wb_ref[...] = w_ref[...].astype(BF16)
    else:
        wb_ref, r_ref, o_ref = refs[n_a:]

    acc = r_ref[...]
    off = 0
    for a_ref in a_refs:
        ka = a_ref.shape[1]
        acc = acc + _dot(a_ref[...], wb_ref[off:off + ka, :])
        off += ka
    o_ref[...] = acc


RESIDENT_W_BYTES = 32 * 1024 * 1024
PIPELINE_MARGIN_BYTES = 4 * 1024 * 1024
MAX_ROW_BLOCK = 512


def _matmul_res(a_list, w_stack, layer, res, *, name):
    m = res.shape[0]
    k, n = w_stack.shape[1:]
    cast_w = w_stack.dtype != BF16
    w_bytes = w_stack.dtype.itemsize + (2 if cast_w else 0)
    tn = n
    while k * tn * w_bytes > RESIDENT_W_BYTES and tn % 2 == 0:
        tn //= 2
    row_bytes = 2 * k * 2 + 4 * tn * 4
    tm = min(MAX_ROW_BLOCK, m)
    while tm * row_bytes > VMEM_LIMIT - k * tn * w_bytes - PIPELINE_MARGIN_BYTES and tm % 2 == 0:
        tm //= 2
    assert m % tm == 0 and n % tn == 0 and sum(a.shape[1] for a in a_list) == k
    return pl.pallas_call(
        functools.partial(_matmul_res_kernel, n_a=len(a_list), cast_w=cast_w),
        out_shape=jax.ShapeDtypeStruct((m, n), F32),
        grid=(n // tn, m // tm),
        in_specs=[pl.BlockSpec((tm, a.shape[1]), lambda j, i: (i, 0)) for a in a_list] + [
            pl.BlockSpec((None, k, tn), lambda j, i: (layer, 0, j), pipeline_mode=pl.Buffered(1)),
            pl.BlockSpec((tm, tn), lambda j, i: (i, j)),
        ],
        out_specs=pl.BlockSpec((tm, tn), lambda j, i: (i, j)),
        scratch_shapes=[pltpu.VMEM((k, tn), BF16)] if cast_w else [],
        compiler_params=_params(("parallel", "arbitrary")),
        name=name,
    )(*a_list, w_stack, res)


def _fox_prep_kernel(gb_ref, gc_ref, xv_ref, q_ref, k_ref, fl_ref, cw_ref, fb_ref, qg_ref, kg_ref,
                     ya_ref, qn_ref, kn_ref, c_ref, ct_ref, qmax_ref, kmax_ref, ubuf, carry, *, head_dim, scale):
    tb = gb_ref.shape[0]
    first = pl.program_id(0) == 0

    @pl.when(first)
    def _():
        ubuf[0:HALO, :] = jnp.zeros((HALO, ubuf.shape[1]), F32)
        carry[...] = jnp.zeros_like(carry)
        qmax_ref[...] = jnp.zeros_like(qmax_ref)
        kmax_ref[...] = jnp.zeros_like(kmax_ref)

    ubuf[HALO:HALO + tb, :] = gc_ref[...].astype(F32) * xv_ref[...].astype(F32)
    y = jnp.zeros((tb, ubuf.shape[1]), F32)
    for j in range(CONV_A_K):
        y = y + cw_ref[j:j + 1, :] * ubuf[pl.ds(HALO - (CONV_A_K - 1) + j, tb), :]
    ya_ref[...] = (gb_ref[...].astype(F32) * y).astype(ya_ref.dtype)
    ubuf[0:HALO, :] = ubuf[tb:tb + HALO, :]

    for h in range(q_ref.shape[1] // head_dim):
        sl = slice(h * head_dim, (h + 1) * head_dim)
        qh = q_ref[:, sl].astype(F32)
        kh = k_ref[:, sl].astype(F32)
        qms = jnp.mean(qh * qh, axis=-1, keepdims=True)
        kms = jnp.mean(kh * kh, axis=-1, keepdims=True)
        qn = (qh * lax.rsqrt(qms + NORM_EPS) * (qg_ref[...] * scale)).astype(qn_ref.dtype)
        kn = (kh * lax.rsqrt(kms + NORM_EPS) * kg_ref[...]).astype(kn_ref.dtype)
        qn_ref[:, sl] = qn
        kn_ref[:, sl] = kn
        qf, kf = qn.astype(F32), kn.astype(F32)
        q_norm = jnp.sqrt(jnp.max(jnp.sum(qf * qf, axis=-1, keepdims=True), axis=0, keepdims=True))
        k_norm = jnp.sqrt(jnp.max(jnp.sum(kf * kf, axis=-1, keepdims=True), axis=0, keepdims=True))
        qmax_ref[h:h + 1, :] = jnp.maximum(qmax_ref[h:h + 1, :], q_norm)
        kmax_ref[h:h + 1, :] = jnp.maximum(kmax_ref[h:h + 1, :], k_norm)

    z = fl_ref[...] + fb_ref[...]
    log_f = jnp.minimum(z, 0.0) - jnp.log1p(jnp.exp(-jnp.abs(z)))
    row = lax.broadcasted_iota(jnp.int32, (tb, tb), 0)
    col = lax.broadcasted_iota(jnp.int32, (tb, tb), 1)
    tri = (col <= row).astype(F32)
    c = jnp.dot(tri, log_f, precision=lax.Precision.HIGHEST, preferred_element_type=F32) + carry[...]
    c_ref[...] = c
    ct_ref[...] = c.T[0:ct_ref.shape[0], :]
    carry[...] = c[tb - 1:tb, :]


def _fox_prep(proj, fl, conv_w_t, f_bias_p, q_norm, k_norm, *, width, tb=512):
    s = proj.shape[0]
    tb = min(tb, s)
    head_dim = width // FOX_HEADS
    col = lambda c: pl.BlockSpec((tb, width), lambda i, c=c: (i, c))
    const = lambda shape: pl.BlockSpec(shape, lambda i: (0, 0))
    return pl.pallas_call(
        functools.partial(_fox_prep_kernel, head_dim=head_dim, scale=LOG2E * head_dim ** -0.5),
        out_shape=[
            jax.ShapeDtypeStruct((s, width), BF16),
            jax.ShapeDtypeStruct((s, width), BF16),
            jax.ShapeDtypeStruct((s, width), BF16),
            jax.ShapeDtypeStruct((s, LANES), F32),
            jax.ShapeDtypeStruct((FOX_HEADS, s), F32),
            jax.ShapeDtypeStruct((FOX_HEADS, LANES), F32),
            jax.ShapeDtypeStruct((FOX_HEADS, LANES), F32),
        ],
        grid=(s // tb,),
        in_specs=[col(0), col(1), col(2), col(3), col(4),
                  pl.BlockSpec((tb, LANES), lambda i: (i, 0)),
                  const((CONV_A_K, width)), const((1, LANES)),
                  const((1, head_dim)), const((1, head_dim))],
        out_specs=[
            pl.BlockSpec((tb, width), lambda i: (i, 0)),
            pl.BlockSpec((tb, width), lambda i: (i, 0)),
            pl.BlockSpec((tb, width), lambda i: (i, 0)),
            pl.BlockSpec((tb, LANES), lambda i: (i, 0)),
            pl.BlockSpec((FOX_HEADS, tb), lambda i: (0, i)),
            const((FOX_HEADS, LANES)), const((FOX_HEADS, LANES)),
        ],
        scratch_shapes=[pltpu.VMEM((tb + HALO, width), F32), pltpu.VMEM((1, LANES), F32)],
        compiler_params=_params(("arbitrary",)),
        name="fox_prep",
    )(proj, proj, proj, proj, proj, fl, conv_w_t, f_bias_p, q_norm.reshape(1, -1), k_norm.reshape(1, -1))


def _fox_attn_kernel(cend_ref, cstart_ref, bound_ref, q_ref, k_ref, v_ref, cq_ref, ck_ref, wf_ref, o_ref, wb_ref,
                     m_ref, l_ref, acc_ref, *, tile):
    wb_ref[...] = wf_ref[...].astype(BF16)
    h = pl.program_id(0)
    qi = pl.program_id(1)
    reps = tile // LANES
    lane = lax.broadcasted_iota(jnp.int32, (tile, LANES), 1)
    cq = jnp.sum(jnp.where(lane == h, cq_ref[...], 0.0), axis=-1, keepdims=True) * LOG2E
    cq = jnp.broadcast_to(cq, (tile, LANES))
    q = q_ref[...]

    def scores(j):
        start = pl.multiple_of(j * tile, tile)
        ck = ck_ref[h, pl.ds(j, 1), :] * LOG2E
        return _dot_nt(q, k_ref[pl.ds(start, tile), :]) - ck, start

    def accumulate(s, start, stab):
        p = jnp.exp2(s - jnp.concatenate([stab] * reps, axis=1))
        part = p[:, 0:LANES]
        for r in range(1, reps):
            part = part + p[:, r * LANES:(r + 1) * LANES]
        l_ref[...] += part
        acc_ref[...] += _dot(p.astype(v_ref.dtype), v_ref[pl.ds(start, tile), :])

    def online_step(j, masked):
        s, start = scores(j)
        if masked:
            q_pos = lax.broadcasted_iota(jnp.int32, (tile, tile), 0)
            k_pos = lax.broadcasted_iota(jnp.int32, (tile, tile), 1)
            s = jnp.where(k_pos <= q_pos, s, -jnp.inf)
        m_old = m_ref[...]
        m_new = jnp.maximum(m_old, jnp.broadcast_to(jnp.max(s, axis=-1, keepdims=True), (tile, LANES)) + cq)
        alpha = jnp.exp2(m_old - m_new)
        l_ref[...] = alpha * l_ref[...]
        acc_ref[...] = alpha * acc_ref[...]
        m_ref[...] = m_new
        accumulate(s, start, m_new - cq)

    m_ref[...] = jnp.full_like(m_ref, -jnp.inf)
    l_ref[...] = jnp.zeros_like(l_ref)
    acc_ref[...] = jnp.zeros_like(acc_ref)
    online_step(qi, True)

    two_b = 2.0 * bound_ref[h]
    reach = two_b + PRUNE_LOG2
    c_tile = cstart_ref[h, qi]

    def in_reach(j):
        return jnp.logical_and(j >= 0, cend_ref[h, jnp.maximum(j, 0)] - c_tile <= reach)

    def fast_block(j):
        s, start = scores(j)
        accumulate(s, start, m_ref[...] - cq)
        return j - 1

    def online_block(j):
        online_step(j, False)
        return j - 1

    @pl.when(two_b <= FAST_LOG2)
    def _():
        lax.while_loop(in_reach, fast_block, qi - 1)

    @pl.when(two_b > FAST_LOG2)
    def _():
        lax.while_loop(in_reach, online_block, qi - 1)

    o_ref[...] = (acc_ref[...] / jnp.sum(l_ref[...], axis=-1, keepdims=True)).astype(o_ref.dtype)


def _fox_attention(qn, kn, proj, v_col0, c_col, c_row, qmax, kmax, cast, *, tile=512):
    s, width = qn.shape
    d = width // FOX_HEADS
    tile = min(tile, s)
    assert d == LANES and s % tile == 0 and tile % LANES == 0
    n_blk = s // tile
    cast_in, cast_out, cast_shape = _cast_through_specs(*cast, (FOX_HEADS, n_blk))
    cend = c_row[:, tile - 1::tile] * LOG2E
    cstart = c_row[:, ::tile] * LOG2E
    bound = qmax[:, 0] * kmax[:, 0] * NORM_MARGIN
    smem = pl.BlockSpec(memory_space=pltpu.SMEM)
    return pl.pallas_call(
        functools.partial(_fox_attn_kernel, tile=tile),
        out_shape=[jax.ShapeDtypeStruct((s, width), BF16), cast_shape],
        grid=(FOX_HEADS, n_blk),
        in_specs=[
            smem, smem, smem,
            pl.BlockSpec((tile, d), lambda h, qi: (qi, h)),
            pl.BlockSpec((s, d), lambda h, qi: (0, h)),
            pl.BlockSpec((s, d), lambda h, qi: (0, v_col0 + h)),
            pl.BlockSpec((tile, LANES), lambda h, qi: (qi, 0)),
            pl.BlockSpec((FOX_HEADS, n_blk, tile), lambda h, qi: (0, 0, 0)),
            cast_in,
        ],
        out_specs=[pl.BlockSpec((tile, d), lambda h, qi: (qi, h)), cast_out],
        scratch_shapes=[pltpu.VMEM((tile, LANES), F32), pltpu.VMEM((tile, LANES), F32),
                        pltpu.VMEM((tile, d), F32)],
        compiler_params=_params(("parallel", "arbitrary")),
        name="fox_attention",
    )(cend, cstart, bound, qn, kn, proj, c_col, c_row.reshape(FOX_HEADS, n_blk, tile), cast[0])


def _gdn_gates_kernel(side_ref, alog_ref, dtb_ref, beta_ref, gc_ref):
    side = side_ref[...]
    tb = side.shape[0]
    beta_ref[...] = _sigmoid(side)
    g = -jnp.exp(alog_ref[...]) * _softplus(side + dtb_ref[...])
    row = lax.broadcasted_iota(jnp.int32, (tb, tb), 0)
    col = lax.broadcasted_iota(jnp.int32, (tb, tb), 1)
    tri = jnp.logical_and(row // GDN_CHUNK == col // GDN_CHUNK, col <= row).astype(F32)
    gc_ref[...] = jnp.dot(tri, g, precision=lax.Precision.HIGHEST, preferred_element_type=F32)


def _gdn_gates(side, alog_p, dtb_p, *, tb=512):
    s = side.shape[0]
    tb = min(tb, s)
    assert s % tb == 0 and tb % GDN_CHUNK == 0
    blk = pl.BlockSpec((tb, LANES), lambda i: (i, 0))
    const = pl.BlockSpec((1, LANES), lambda i: (0, 0))
    return pl.pallas_call(
        _gdn_gates_kernel,
        out_shape=[jax.ShapeDtypeStruct((s, LANES), F32), jax.ShapeDtypeStruct((s, LANES), F32)],
        grid=(s // tb,),
        in_specs=[blk, const, const],
        out_specs=[blk, blk],
        compiler_params=_params(("parallel",)),
        name="gdn_gates",
    )(side, alog_p, dtb_p)


def _conv_silu(buf, raw_ref, w_ref, tc):
    buf[HALO:HALO + tc, :] = raw_ref[...].astype(F32)
    y = jnp.zeros((tc, buf.shape[1]), F32)
    for j in range(GDN_CONV_K):
        y = y + w_ref[j:j + 1, :] * buf[pl.ds(HALO - (GDN_CONV_K - 1) + j, tc), :]
    buf[0:HALO, :] = buf[tc:tc + HALO, :]
    return _silu(y)


def _l2_norm(x):
    return x * lax.rsqrt(jnp.sum(x * x, axis=-1, keepdims=True) + NORM_EPS)


def _unit_lower_inverses(l_mats, eye, row, col):
    n = eye.shape[0]
    same_base = (row // INV_BASE) == (col // INV_BASE)
    lbs = [jnp.where(same_base, l, 0.0).astype(BF16) for l in l_mats]
    ps = [eye - lb.astype(F32) for lb in lbs]
    ms = [_dot(lb, lb) for lb in lbs]
    power = 2
    while power < INV_BASE:
        mbs = [m.astype(BF16) for m in ms]
        if 2 * power < INV_BASE:
            boths = [_dot(mb, jnp.concatenate([mb, p.astype(BF16)], axis=1)) for mb, p in zip(mbs, ps)]
            ms = [b[:, :n] for b in boths]
            xs = [b[:, n:] for b in boths]
        else:
            xs = [_dot(mb, p.astype(BF16)) for mb, p in zip(mbs, ps)]
        ps = [p + x for p, x in zip(ps, xs)]
        power *= 2
    size = INV_BASE
    while size < GDN_CHUNK:
        below = jnp.logical_and(row // (2 * size) == col // (2 * size), row // size == col // size + 1)
        c_offs = [jnp.where(below, l, 0.0).astype(BF16) for l in l_mats]
        pbs = [p.astype(BF16) for p in ps]
        ts = [_dot(c, pb).astype(BF16) for c, pb in zip(c_offs, pbs)]
        ps = [p - _dot(pb, t) for p, pb, t in zip(ps, pbs, ts)]
        size *= 2
    return ps


def _gdn_prep_kernel(q_ref, k_ref, v_ref, beta_ref, gc_ref, wq_ref, wk_ref, wv_ref, wf_ref,
                     g_out, n_out, qp_out, op_out, eg_out, wb_ref, qbuf, kbuf, vbuf, *, n_vheads):
    wb_ref[...] = wf_ref[...].astype(BF16)
    tc = q_ref.shape[0]
    d = GDN_HEAD_DIM
    hq = pl.program_id(0)
    rep = v_ref.shape[1] // d
    chunks_per_group = GROUP // GDN_CHUNK

    @pl.when(pl.program_id(1) == 0)
    def _():
        qbuf[0:HALO, :] = jnp.zeros((HALO, qbuf.shape[1]), F32)
        kbuf[0:HALO, :] = jnp.zeros((HALO, kbuf.shape[1]), F32)
        vbuf[0:HALO, :] = jnp.zeros((HALO, vbuf.shape[1]), F32)

    q = _l2_norm(_conv_silu(qbuf, q_ref, wq_ref, tc)) * (d ** -0.5)
    k = _l2_norm(_conv_silu(kbuf, k_ref, wk_ref, tc))
    v = _conv_silu(vbuf, v_ref, wv_ref, tc)

    beta_all = beta_ref[...]
    gc_all = gc_ref[...]
    lane = lax.broadcasted_iota(jnp.int32, beta_all.shape, 1)
    row = lax.broadcasted_iota(jnp.int32, (GROUP, GROUP), 0)
    col = lax.broadcasted_iota(jnp.int32, (GROUP, GROUP), 1)
    same_chunk = (row // GDN_CHUNK) == (col // GDN_CHUNK)
    incl = jnp.logical_and(same_chunk, col <= row)
    strict = jnp.logical_and(same_chunk, col < row)
    eye = (row == col).astype(F32)

    heads = []
    for j in range(rep):
        vh = hq * rep + j
        beta = jnp.sum(jnp.where(lane == vh, beta_all, 0.0), axis=-1, keepdims=True)
        gc = jnp.sum(jnp.where(lane == n_vheads + vh, gc_all, 0.0), axis=-1, keepdims=True)
        heads.append((beta, gc))

    probs = [(j, r) for j in range(rep) for r in range(tc // GROUP)]
    rows_of = lambda r: slice(r * GROUP, (r + 1) * GROUP)
    kgs = [k[rows_of(r)] for _, r in probs]
    qgs = [q[rows_of(r)] for _, r in probs]
    kgbs = [kg.astype(BF16) for kg in kgs]
    bgs = [heads[j][0][rows_of(r)] for j, r in probs]
    gcbs = [jnp.broadcast_to(heads[j][1][rows_of(r)], (GROUP, GROUP)) for j, r in probs]
    gcrs = [gcb.T for gcb in gcbs]
    decays = [jnp.where(incl, jnp.exp(gcb - gcr), 0.0) for gcb, gcr in zip(gcbs, gcrs)]
    kbs = [kg * bg for kg, bg in zip(kgs, bgs)]
    kqs = [_dot_nt(jnp.concatenate([kb, qg], axis=0).astype(BF16), kgb) for kb, qg, kgb in zip(kbs, qgs, kgbs)]
    l_mats = [jnp.where(strict, kq[:GROUP] * dec, 0.0) for kq, dec in zip(kqs, decays)]
    scores = [(kq[GROUP:] * dec).astype(BF16) for kq, dec in zip(kqs, decays)]
    t_invs = _unit_lower_inverses(l_mats, eye, row, col)
    egcs = [jnp.exp(gcb) for gcb in gcbs]
    rhs = [jnp.concatenate([(v[rows_of(r), j * d:(j + 1) * d] * bg).astype(BF16), (kb * egc).astype(BF16)], axis=1)
           for (j, r), bg, kb, egc in zip(probs, bgs, kbs, egcs)]
    uws = [_dot(t.astype(BF16), x) for t, x in zip(t_invs, rhs)]
    wus = [jnp.concatenate([uw[:, d:], uw[:, :d]], axis=1).astype(BF16) for uw in uws]
    g_lasts = [jnp.where(row < GDN_CHUNK, gcb[GDN_CHUNK - 1:GDN_CHUNK, :], gcb[GROUP - 1:GROUP, :]) for gcb in gcbs]
    kdts = [(kg * jnp.exp(gl - gcb)).T for kg, gl, gcb in zip(kgs, g_lasts, gcbs)]
    qos = [_dot(sc, wu) for sc, wu in zip(scores, wus)]
    gns = [[_dot(jnp.where(col // GDN_CHUNK == c, kdt, 0.0).astype(BF16), wu) for c in range(chunks_per_group)]
           for kdt, wu in zip(kdts, wus)]

    for p, (j, r) in enumerate(probs):
        cols = slice(j * d, (j + 1) * d)
        qp_out[rows_of(r), cols] = (qgs[p] * egcs[p] - qos[p][:, :d]).astype(qp_out.dtype)
        op_out[rows_of(r), cols] = qos[p][:, d:].astype(op_out.dtype)
        for c in range(chunks_per_group):
            chunk = r * chunks_per_group + c
            g_out[j, chunk * d:(chunk + 1) * d, :] = gns[p][c][:, :d].astype(g_out.dtype)
            n_out[j, chunk * d:(chunk + 1) * d, :] = gns[p][c][:, d:].astype(n_out.dtype)
            last = (c + 1) * GDN_CHUNK - 1
            eg_out[j, chunk:chunk + 1, :] = jnp.exp(gcbs[p][last:last + 1, :])


def _gdn_prep(proj, beta, gc, conv_w_t, cast, *, n_qk, n_v, tc=512):
    s = proj.shape[0]
    d = GDN_HEAD_DIM
    rep = n_v // n_qk
    tc = min(tc, s)
    assert s % tc == 0 and tc % GROUP == 0
    kw = n_qk * d
    v_blk0 = 2 * kw // (rep * d)
    n_chunks = s // GDN_CHUNK
    cpb = tc // GDN_CHUNK
    cast_in, cast_out, cast_shape = _cast_through_specs(*cast, (n_qk, s // tc))
    return pl.pallas_call(
        functools.partial(_gdn_prep_kernel, n_vheads=n_v),
        out_shape=[
            jax.ShapeDtypeStruct((n_v, n_chunks * d, d), BF16),
            jax.ShapeDtypeStruct((n_v, n_chunks * d, d), BF16),
            jax.ShapeDtypeStruct((s, n_v * d), BF16),
            jax.ShapeDtypeStruct((s, n_v * d), BF16),
            jax.ShapeDtypeStruct((n_v, n_chunks, LANES), F32),
            cast_shape,
        ],
        grid=(n_qk, s // tc),
        in_specs=[
            pl.BlockSpec((tc, d), lambda h, i: (i, h)),
            pl.BlockSpec((tc, d), lambda h, i: (i, n_qk + h)),
            pl.BlockSpec((tc, rep * d), lambda h, i: (i, v_blk0 + h)),
            pl.BlockSpec((tc, LANES), lambda h, i: (i, 0)),
            pl.BlockSpec((tc, LANES), lambda h, i: (i, 0)),
            pl.BlockSpec((GDN_CONV_K, d), lambda h, i: (0, h)),
            pl.BlockSpec((GDN_CONV_K, d), lambda h, i: (0, n_qk + h)),
            pl.BlockSpec((GDN_CONV_K, rep * d), lambda h, i: (0, v_blk0 + h)),
            cast_in,
        ],
        out_specs=[
            pl.BlockSpec((rep, cpb * d, d), lambda h, i: (h, i, 0)),
            pl.BlockSpec((rep, cpb * d, d), lambda h, i: (h, i, 0)),
            pl.BlockSpec((tc, rep * d), lambda h, i: (i, h)),
            pl.BlockSpec((tc, rep * d), lambda h, i: (i, h)),
            pl.BlockSpec((rep, cpb, LANES), lambda h, i: (h, i, 0)),
            cast_out,
        ],
        scratch_shapes=[pltpu.VMEM((tc + HALO, d), F32), pltpu.VMEM((tc + HALO, d), F32),
                        pltpu.VMEM((tc + HALO, rep * d), F32)],
        compiler_params=_params(("parallel", "arbitrary")),
        name="gdn_prep",
    )(proj, proj, proj, beta, gc, conv_w_t, conv_w_t, conv_w_t, cast[0])


def _gdn_rec_kernel(g_ref, n_ref, qp_ref, op_ref, eg_ref, z_ref, on_ref, o_ref, state, obuf):
    hb = g_ref.shape[0]
    tr = qp_ref.shape[0]
    d = GDN_HEAD_DIM

    @pl.when(pl.program_id(1) == 0)
    def _():
        state[...] = jnp.zeros_like(state)

    def chunk_step(c, carry):
        r0 = pl.multiple_of(c * d, d)
        q0 = pl.multiple_of(c * GDN_CHUNK, GDN_CHUNK)
        sbs = [state[j].astype(BF16) for j in range(hb)]
        gs = [_dot(g_ref[j, pl.ds(r0, d), :], sbs[j]) for j in range(hb)]
        os = [_dot(qp_ref[pl.ds(q0, GDN_CHUNK), j * d:(j + 1) * d], sbs[j]) for j in range(hb)]
        for j in range(hb):
            state[j] = eg_ref[j, pl.ds(c, 1), :] * state[j] - gs[j] + n_ref[j, pl.ds(r0, d), :].astype(F32)
            obuf[pl.ds(q0, GDN_CHUNK), j * d:(j + 1) * d] = (
                os[j] + op_ref[pl.ds(q0, GDN_CHUNK), j * d:(j + 1) * d].astype(F32))
        return carry

    lax.fori_loop(0, tr // GDN_CHUNK, chunk_step, 0)

    for j in range(hb):
        cols = slice(j * d, (j + 1) * d)
        o = obuf[:, cols]
        ms = jnp.mean(o * o, axis=-1, keepdims=True)
        o_ref[:, cols] = (o * lax.rsqrt(ms + NORM_EPS) * on_ref[...] * _silu(z_ref[:, cols].astype(F32))
                          ).astype(o_ref.dtype)


def _gdn_recurrence(g_op, n_op, qp, op, eg, proj, z_col0, out_norm, *, hb=16, tr=512):
    n_v, rows, d = g_op.shape
    s = qp.shape[0]
    hb = min(hb, n_v)
    tr = min(tr, s)
    assert n_v % hb == 0 and s % tr == 0 and z_col0 % (hb * d) == 0
    cpb = tr // GDN_CHUNK
    z_blk0 = z_col0 // (hb * d)
    return pl.pallas_call(
        _gdn_rec_kernel,
        out_shape=jax.ShapeDtypeStruct((s, n_v * d), BF16),
        grid=(n_v // hb, s // tr),
        in_specs=[
            pl.BlockSpec((hb, cpb * d, d), lambda h, i: (h, i, 0)),
            pl.BlockSpec((hb, cpb * d, d), lambda h, i: (h, i, 0)),
            pl.BlockSpec((tr, hb * d), lambda h, i: (i, h)),
            pl.BlockSpec((tr, hb * d), lambda h, i: (i, h)),
            pl.BlockSpec((hb, cpb, LANES), lambda h, i: (h, i, 0)),
            pl.BlockSpec((tr, hb * d), lambda h, i: (i, z_blk0 + h)),
            pl.BlockSpec((1, d), lambda h, i: (0, 0)),
        ],
        out_specs=pl.BlockSpec((tr, hb * d), lambda h, i: (i, h)),
        scratch_shapes=[pltpu.VMEM((hb, d, d), F32), pltpu.VMEM((tr, hb * d), F32)],
        compiler_params=_params(("parallel", "arbitrary")),
        name="gdn_recurrence",
    )(g_op, n_op, qp, op, eg, proj, out_norm.reshape(1, d))


def _pad_lanes(vec, offset=0):
    return jnp.zeros((1, LANES), F32).at[0, offset:offset + vec.shape[0]].set(vec.astype(F32))


def kernel(x, norm_mix, norm_mlp, w_mlp_in, w_mlp_out, ab_w_in, ab_conv_w, fox_f_bias, fox_q_norm,
           fox_k_norm, ab_w_out, gdn_w_in, gdn_conv_w, gdn_a_log, gdn_dt_bias, gdn_out_norm, gdn_w_out):
    bsz, s_len, d_model = x.shape
    depth = norm_mix.shape[0]
    width = ab_conv_w.shape[1]
    n_v = gdn_a_log.shape[1]
    qkv = gdn_conv_w.shape[1]
    n_qk = (qkv - n_v * GDN_HEAD_DIM) // (2 * GDN_HEAD_DIM)
    ab_main = 6 * width
    gdn_main = qkv + n_v * GDN_HEAD_DIM
    d_ff = w_mlp_in.shape[2]
    assert ab_w_in.shape[2] == ab_main + FOX_HEADS and gdn_w_in.shape[2] == gdn_main + 2 * n_v
    assert 2 * n_v <= LANES and FOX_HEADS <= LANES

    def side_weight(w_stack, layer, start):
        cols = lax.slice(w_stack, (layer, 0, start), (layer + 1, w_stack.shape[1], w_stack.shape[2]))[0]
        return jnp.pad(cols, ((0, 0), (0, LANES - cols.shape[1])))

    ab_w_main = ab_w_in[:, :, :ab_main].astype(BF16)
    gdn_w_main = gdn_w_in[:, :, :gdn_main].astype(BF16)

    outs = []
    for b in range(bsz):
        xb = x[b]
        for layer in range(depth):
            i = layer // 2
            if layer % 2 == 0:
                proj, fl = _norm_matmul(xb, norm_mix[layer], ab_w_main, i, ab_main,
                                        side_w=side_weight(ab_w_in, i, ab_main), name="ab_in_proj")
                y_a, qn, kn, c_col, c_row, qmax, kmax = _fox_prep(
                    proj, fl, ab_conv_w[i].T, _pad_lanes(fox_f_bias[i]), fox_q_norm[i], fox_k_norm[i], width=width)
                y_b, w_out_b = _fox_attention(qn, kn, proj, 5 * FOX_HEADS, c_col, c_row, qmax, kmax,
                                              (w_mlp_out, layer))
                xb = _matmul_res([y_a, y_b], ab_w_out, i, xb, name="ab_out_proj")
            else:
                proj, side = _norm_matmul(xb, norm_mix[layer], gdn_w_main, i, gdn_main,
                                          side_w=side_weight(gdn_w_in, i, gdn_main), name="gdn_in_proj")
                beta, gc = _gdn_gates(side, _pad_lanes(gdn_a_log[i], n_v), _pad_lanes(gdn_dt_bias[i], n_v))
                g_op, n_op, qp, op, eg, w_out_b = _gdn_prep(proj, beta, gc, gdn_conv_w[i].T, (w_mlp_out, layer),
                                                            n_qk=n_qk, n_v=n_v)
                o = _gdn_recurrence(g_op, n_op, qp, op, eg, proj, qkv, gdn_out_norm[i])
                xb = _matmul_res([o], gdn_w_out, i, xb, name="gdn_out_proj")
            hidden = _norm_matmul(xb, norm_mlp[layer], w_mlp_in, layer, d_ff, act=True, name="mlp_in")
            xb = _matmul_res([hidden], w_out_b, 0, xb, name="mlp_out")
        outs.append(xb)
    return outs[0][None] if bsz == 1 else jnp.stack(outs, axis=0)
```
